```python
import numpy as np
import jax
import jax.numpy as jnp
from jax import lax

D_MODEL = 1024
BATCH = 32
SEQ = 2048
DEPTH = 2
DEC_BATCH = 8
DEC_SEQ = 8192
PAST_LEN = 128

GROUP_WIDTH = D_MODEL // 4
DN_HEADS = 4
DN_DK = GROUP_WIDTH // DN_HEADS
DN_DV = GROUP_WIDTH // DN_HEADS
DN_CONV = 5
DN_CHUNK = 64
SGU_GROUPS = 4
SGU_CHUNK = 128
SGU_WIDTH = GROUP_WIDTH
POOL_WINDOWS = (2, 4, 8, 16)
POOL_GW = GROUP_WIDTH // len(POOL_WINDOWS)
MLA_HEADS = 4
MLA_NOPE = 64
MLA_ROPE = 32
MLA_V = GROUP_WIDTH // MLA_HEADS
MLA_Q_RANK = 384
MLA_KV_RANK = 128
ROPE_THETA = 10000.0
Q_BLOCK = 128
N_EXPERTS = 16
N_EXPERT_GROUPS = 4
TOP_K = 2
EXPERT_FF = 512
DEEPNORM_ALPHA = (2 * DEPTH) ** 0.25
DEEPNORM_BETA = (8 * DEPTH) ** -0.25
LN_EPS = 1e-5
RMS_EPS = 1e-6
IN_SPLITS = (3 * GROUP_WIDTH, GROUP_WIDTH, 2 * DN_HEADS, 2 * DN_HEADS, 2 * SGU_WIDTH, GROUP_WIDTH, MLA_Q_RANK, MLA_KV_RANK, MLA_ROPE)
IN_WIDTH = sum(IN_SPLITS)

kernel_name = "hymba_style_bidir_hybrid_encoder"


def layer_norm(x, g, b):
    xf = x.astype(jnp.float32)
    mu = jnp.mean(xf, axis=-1, keepdims=True)
    var = jnp.mean(jnp.square(xf - mu), axis=-1, keepdims=True)
    y = (xf - mu) * lax.rsqrt(var + LN_EPS) * g.astype(jnp.float32) + b.astype(jnp.float32)
    return y.astype(x.dtype)


def rms_norm(x, g):
    xf = x.astype(jnp.float32)
    y = xf * lax.rsqrt(jnp.mean(xf * xf, axis=-1, keepdims=True) + RMS_EPS) * g.astype(jnp.float32)
    return y.astype(x.dtype)


def l2_normalize(x):
    return x * lax.rsqrt(jnp.sum(x * x, axis=-1, keepdims=True) + RMS_EPS)


def centred_depthwise_conv(x, w):
    k, c = w.shape
    return lax.conv_general_dilated(x, w.astype(x.dtype)[:, None, :], (1,), [(k // 2, k // 2)],
                                    dimension_numbers=('NWC', 'WIO', 'NWC'), feature_group_count=c)


def chunked_gated_delta_rule(q, k, v, g, beta):
    bsz, s, h, dk = q.shape
    dv = v.shape[-1]
    n = s // DN_CHUNK

    def to_chunks(t):
        t = jnp.moveaxis(t, 2, 1)
        return t.reshape(bsz, h, n, DN_CHUNK, *t.shape[3:])

    q, k, v, g, beta = (to_chunks(t) for t in (q, k, v, g, beta))
    gc = jnp.cumsum(g, axis=-1)
    incl = jnp.tril(jnp.ones((DN_CHUNK, DN_CHUNK), dtype=bool))
    strict = jnp.tril(jnp.ones((DN_CHUNK, DN_CHUNK), dtype=bool), -1)
    diff = gc[..., :, None] - gc[..., None, :]
    decay = jnp.where(incl, jnp.exp(jnp.where(incl, diff, 0.0)), 0.0)
    kb = k * beta[..., None]
    lower = jnp.where(strict, jnp.einsum('bhncd,bhnjd->bhncj', kb, k) * decay, 0.0)
    eye = jnp.eye(DN_CHUNK, dtype=q.dtype)
    rhs = jnp.concatenate([v * beta[..., None], kb * jnp.exp(gc)[..., None]], axis=-1)
    sol = lax.linalg.triangular_solve(eye + lower, rhs, left_side=True, lower=True, unit_diagonal=True)
    u, w = sol[..., :dv], sol[..., dv:]
    qk = jnp.einsum('bhncd,bhnjd->bhncj', q, k) * decay
    q_dec = q * jnp.exp(gc)[..., None]
    k_dec = k * jnp.exp(gc[..., -1:] - gc)[..., None]
    g_tot = jnp.exp(gc[..., -1])

    def step(state, inp):
        qd, kd, uu, ww, a, gt = inp
        v_new = uu - jnp.einsum('bhcd,bhde->bhce', ww, state)
        out = jnp.einsum('bhcd,bhde->bhce', qd, state) + jnp.einsum('bhcj,bhje->bhce', a, v_new)
        state = state * gt[..., None, None] + jnp.einsum('bhcd,bhce->bhde', kd, v_new)
        return state, out

    xs = tuple(jnp.moveaxis(t, 2, 0) for t in (q_dec, k_dec, u, w, qk, g_tot))
    state0 = jnp.zeros((bsz, h, dk, dv), q.dtype)
    _, o = lax.scan(step, state0, xs)
    o = jnp.moveaxis(o, 0, 2).reshape(bsz, h, s, dv)
    return jnp.moveaxis(o, 1, 2)


def gated_deltanet(qkv, z, b_proj, a_proj, conv_w, a_log, dt_bias, norm_g):
    bsz, s, _ = qkv.shape
    f32 = jnp.float32
    qkv = jax.nn.silu(centred_depthwise_conv(qkv, conv_w)).astype(f32)
    q, k, v = jnp.split(qkv, 3, axis=-1)
    q = l2_normalize(q.reshape(bsz, s, DN_HEADS, DN_DK)) * (DN_DK ** -0.5)
    k = l2_normalize(k.reshape(bsz, s, DN_HEADS, DN_DK))
    v = v.reshape(bsz, s, DN_HEADS, DN_DV)
    beta = jax.nn.sigmoid(b_proj.astype(f32)).reshape(bsz, s, 2, DN_HEADS)
    g = -jnp.exp(a_log.astype(f32)) * jax.nn.softplus(a_proj.astype(f32).reshape(bsz, s, 2, DN_HEADS) + dt_bias.astype(f32))
    o_fwd = chunked_gated_delta_rule(q, k, v, g[:, :, 0], beta[:, :, 0])
    rev = lambda t: jnp.flip(t, axis=1)
    o_bwd = rev(chunked_gated_delta_rule(rev(q), rev(k), rev(v), rev(g[:, :, 1]), rev(beta[:, :, 1])))
    o = o_fwd + o_bwd
    o = o * lax.rsqrt(jnp.mean(o * o, axis=-1, keepdims=True) + RMS_EPS) * norm_g.astype(f32)
    o = o * jax.nn.silu(z.astype(f32).reshape(bsz, s, DN_HEADS, DN_DV))
    return o.reshape(bsz, s, GROUP_WIDTH).astype(z.dtype)


def spatial_gating(x, ln_g, ln_b, w_s, b_s):
    bsz, s, _ = x.shape
    x = jax.nn.gelu(x)
    u, v = jnp.split(x, 2, axis=-1)
    v = layer_norm(v, ln_g, ln_b)
    n = s // SGU_CHUNK
    v = v.reshape(bsz, n, SGU_CHUNK, SGU_GROUPS, SGU_WIDTH // SGU_GROUPS)
    v = jnp.einsum('gij,bnjgc->bnigc', w_s.astype(v.dtype), v) + b_s.T.astype(v.dtype)[None, None, :, :, None]
    return u * v.reshape(bsz, s, SGU_WIDTH)


def multiscale_pool(x, w_pool, scale):
    bsz, s, _ = x.shape
    xf = x.astype(jnp.float32)
    prefix = jnp.pad(jnp.cumsum(xf, axis=1), ((0, 0), (1, 0), (0, 0)))
    t = jnp.arange(s)
    outs = []
    for gi, win in enumerate(POOL_WINDOWS):
        lo = jnp.clip(t - win // 2, 0, s)
        hi = jnp.clip(t + win // 2, 0, s)
        pg = prefix[..., gi * POOL_GW:(gi + 1) * POOL_GW]
        wsum = jnp.take(pg, hi, axis=1) - jnp.take(pg, lo, axis=1)
        cnt = (hi - lo).astype(jnp.float32)[None, :, None]
        outs.append(wsum / cnt - xf[..., gi * POOL_GW:(gi + 1) * POOL_GW])
    pooled = jnp.concatenate(outs, axis=-1).reshape(bsz, s, len(POOL_WINDOWS), POOL_GW)
    y = jnp.einsum('bsgc,gcd->bsgd', pooled, w_pool.astype(jnp.float32)).reshape(bsz, s, GROUP_WIDTH)
    return (y * scale.astype(jnp.float32)).astype(x.dtype)


def rotary(x):
    s, r = x.shape[1], x.shape[-1]
    inv = ROPE_THETA ** (-jnp.arange(0, r, 2, dtype=jnp.float32) / r)
    ang = jnp.arange(s, dtype=jnp.float32)[:, None] * inv[None, :]
    cos = jnp.cos(ang)[None, :, None, :]
    sin = jnp.sin(ang)[None, :, None, :]
    x1, x2 = jnp.split(x.astype(jnp.float32), 2, axis=-1)
    return jnp.concatenate([x1 * cos - x2 * sin, x2 * cos + x1 * sin], axis=-1).astype(x.dtype)


def latent_attention(c_q, c_kv, k_rope, q_norm_g, kv_norm_g, w_uq, w_uk, w_uv):
    bsz, s, _ = c_q.shape
    q = jnp.einsum('bsr,rhd->bshd', rms_norm(c_q, q_norm_g), w_uq)
    q_nope, q_rope = q[..., :MLA_NOPE], rotary(q[..., MLA_NOPE:])
    c_kv = rms_norm(c_kv, kv_norm_g)
    k_nope = jnp.einsum('bsr,rhd->bshd', c_kv, w_uk)
    v = jnp.einsum('bsr,rhd->bshd', c_kv, w_uv)
    k_r = rotary(k_rope[:, :, None, :])[:, :, 0, :]
    scale = (MLA_NOPE + MLA_ROPE) ** -0.5
    nb = s // Q_BLOCK

    def blocks(t):
        return jnp.moveaxis(t.reshape(bsz, nb, Q_BLOCK, *t.shape[2:]), 1, 0)

    def attend(qs):
        qn, qr = qs
        sc = (jnp.einsum('bqhd,bkhd->bhqk', qn, k_nope).astype(jnp.float32)
              + jnp.einsum('bqhr,bkr->bhqk', qr, k_r).astype(jnp.float32))
        p = jax.nn.softmax(sc * scale, axis=-1).astype(v.dtype)
        return jnp.einsum('bhqk,bkhd->bqhd', p, v)

    o = lax.map(attend, (blocks(q_nope), blocks(q_rope)))
    return jnp.moveaxis(o, 0, 1).reshape(bsz, s, GROUP_WIDTH)


def grouped_moe(x, router_w, router_bias, w_gate, w_up, w_down):
    bsz, s, d = x.shape
    t = x.reshape(-1, d)
    scores = jax.nn.sigmoid((t @ router_w).astype(jnp.float32))
    biased = scores + router_bias.astype(jnp.float32)
    per_group = N_EXPERTS // N_EXPERT_GROUPS
    group_score = jnp.sum(lax.top_k(biased.reshape(-1, N_EXPERT_GROUPS, per_group), TOP_K)[0], axis=-1)
    group = jnp.argmax(group_score, axis=-1)
    in_group = (jnp.arange(N_EXPERTS) // per_group)[None, :] == group[:, None]
    _, idx = lax.top_k(jnp.where(in_group, biased, -jnp.inf), TOP_K)
    wts = jnp.take_along_axis(scores, idx, axis=-1)
    wts = wts / jnp.sum(wts, axis=-1, keepdims=True)
    gate = jnp.sum(jax.nn.one_hot(idx, N_EXPERTS, dtype=jnp.float32) * wts[..., None], axis=1).astype(x.dtype)
    y = jnp.zeros_like(t)
    for e in range(N_EXPERTS):
        h = jax.nn.silu(t @ w_gate[e]) * (t @ w_up[e])
        y = y + gate[:, e:e + 1] * (h @ w_down[e])
    return y.reshape(bsz, s, d)


def setup_inputs(seed: int = 0) -> dict:
    key = jax.random.key(seed)
    keys = iter(jax.random.split(key, 40))
    nrm = lambda shape: jax.random.normal(next(keys), shape, jnp.float32)
    gain = lambda shape: 1.0 + 0.02 * nrm(shape)
    L, D, E, F = DEPTH, D_MODEL, N_EXPERTS, EXPERT_FF
    x_prompt = nrm((BATCH, SEQ, D))
    x_sample = nrm((DEC_BATCH, DEC_SEQ, D))
    ln_in_g = gain((D,))
    ln_in_b = 0.02 * nrm((D,))
    w_in = nrm((L, D, IN_WIDTH)) * D ** -0.5
    dn_conv_w = nrm((L, DN_CONV, 3 * GROUP_WIDTH)) * DN_CONV ** -0.5
    dn_a_log = jnp.log(jax.random.uniform(next(keys), (L, 2, DN_HEADS), jnp.float32, 1.0, 16.0))
    dt = jnp.exp(jax.random.uniform(next(keys), (L, 2, DN_HEADS), jnp.float32, np.log(1e-3), np.log(0.1)))
    dn_dt_bias = dt + jnp.log(-jnp.expm1(-dt))
    dn_norm_g = gain((L, DN_DV))
    sgu_ln_g = gain((L, SGU_WIDTH))
    sgu_ln_b = 0.02 * nrm((L, SGU_WIDTH))
    sgu_w = nrm((L, SGU_GROUPS, SGU_CHUNK, SGU_CHUNK)) * (0.5 * SGU_CHUNK ** -0.5)
    sgu_b = gain((L, SGU_GROUPS, SGU_CHUNK))
    pool_w = nrm((L, len(POOL_WINDOWS), POOL_GW, POOL_GW)) * POOL_GW ** -0.5
    pool_scale = gain((L, GROUP_WIDTH))
    mla_q_norm_g = gain((L, MLA_Q_RANK))
    mla_kv_norm_g = gain((L, MLA_KV_RANK))
    mla_w_uq = nrm((L, MLA_Q_RANK, MLA_HEADS, MLA_NOPE + MLA_ROPE)) * MLA_Q_RANK ** -0.5
    mla_w_uk = nrm((L, MLA_KV_RANK, MLA_HEADS, MLA_NOPE)) * MLA_KV_RANK ** -0.5
    mla_w_uv = nrm((L, MLA_KV_RANK, MLA_HEADS, MLA_V)) * MLA_KV_RANK ** -0.5
    w_out = nrm((L, D, D)) * (D ** -0.5 * DEEPNORM_BETA)
    ln1_g = gain((L, D))
    ln1_b = 0.02 * nrm((L, D))
    router_w = nrm((D, E)) * D ** -0.5
    router_bias = 0.01 * nrm((E,))
    moe_w_gate = nrm((L, E, D, F)) * D ** -0.5
    moe_w_up = nrm((L, E, D, F)) * D ** -0.5
    moe_w_down = nrm((L, E, F, D)) * (F ** -0.5 * DEEPNORM_BETA)
    ln2_g = gain((L, D))
    ln2_b = 0.02 * nrm((L, D))
    return {"x_prompt": x_prompt, "x_sample": x_sample, "ln_in_g": ln_in_g, "ln_in_b": ln_in_b,
            "w_in": w_in, "dn_conv_w": dn_conv_w, "dn_a_log": dn_a_log, "dn_dt_bias": dn_dt_bias,
            "dn_norm_g": dn_norm_g, "sgu_ln_g": sgu_ln_g, "sgu_ln_b": sgu_ln_b, "sgu_w": sgu_w,
            "sgu_b": sgu_b, "pool_w": pool_w, "pool_scale": pool_scale, "mla_q_norm_g": mla_q_norm_g,
            "mla_kv_norm_g": mla_kv_norm_g, "mla_w_uq": mla_w_uq, "mla_w_uk": mla_w_uk, "mla_w_uv": mla_w_uv,
            "w_out": w_out, "ln1_g": ln1_g, "ln1_b": ln1_b, "router_w": router_w, "router_bias": router_bias,
            "moe_w_gate": moe_w_gate, "moe_w_up": moe_w_up, "moe_w_down": moe_w_down,
            "ln2_g": ln2_g, "ln2_b": ln2_b}


def reference(x_prompt, x_sample, ln_in_g, ln_in_b, w_in, dn_conv_w, dn_a_log, dn_dt_bias, dn_norm_g,
              sgu_ln_g, sgu_ln_b, sgu_w, sgu_b, pool_w, pool_scale, mla_q_norm_g, mla_kv_norm_g,
              mla_w_uq, mla_w_uk, mla_w_uv, w_out, ln1_g, ln1_b, router_w, router_bias,
              moe_w_gate, moe_w_up, moe_w_down, ln2_g, ln2_b):
    split_points = np.cumsum(IN_SPLITS)[:-1].tolist()

    def trunk(x):
        x = layer_norm(x, ln_in_g, ln_in_b)
        for l in range(DEPTH):
            proj = x @ w_in[l]
            dn_qkv, dn_z, dn_b, dn_a, sgu_in, pool_in, c_q, c_kv, k_rope = jnp.split(proj, split_points, axis=-1)
            y_a = gated_deltanet(dn_qkv, dn_z, dn_b, dn_a, dn_conv_w[l], dn_a_log[l], dn_dt_bias[l], dn_norm_g[l])
            y_b = spatial_gating(sgu_in, sgu_ln_g[l], sgu_ln_b[l], sgu_w[l], sgu_b[l])
            y_c = multiscale_pool(pool_in, pool_w[l], pool_scale[l])
            y_d = latent_attention(c_q, c_kv, k_rope, mla_q_norm_g[l], mla_kv_norm_g[l],
                                   mla_w_uq[l], mla_w_uk[l], mla_w_uv[l])
            mix = jnp.concatenate([y_a, y_b, y_c, y_d], axis=-1) @ w_out[l]
            x = layer_norm(DEEPNORM_ALPHA * x + mix, ln1_g[l], ln1_b[l])
            ffn = grouped_moe(x, router_w, router_bias, moe_w_gate[l], moe_w_up[l], moe_w_down[l])
            x = layer_norm(DEEPNORM_ALPHA * x + ffn, ln2_g[l], ln2_b[l])
        return x

    y_prompt = trunk(x_prompt)
    y_sample = trunk(x_sample)
    return (y_prompt, y_sample)
```

```python
import functools

import numpy as np
import jax
import jax.numpy as jnp
from jax import lax
from jax.experimental import pallas as pl
from jax.experimental.pallas import tpu as pltpu

F32 = jnp.float32
BF16 = jnp.bfloat16

D_MODEL = 1024
DEPTH = 2
GROUP_WIDTH = D_MODEL // 4
DN_HEADS = 4
DN_DK = GROUP_WIDTH // DN_HEADS
DN_CONV = 5
DN_CHUNK = 64
SGU_GROUPS = 4
SGU_CHUNK = 128
SGU_WIDTH = GROUP_WIDTH
POOL_WINDOWS = (2, 4, 8, 16)
POOL_GW = GROUP_WIDTH // len(POOL_WINDOWS)
MLA_HEADS = 4
MLA_NOPE = 64
MLA_ROPE = 32
MLA_V = GROUP_WIDTH // MLA_HEADS
MLA_Q_RANK = 384
MLA_KV_RANK = 128
ROPE_THETA = 10000.0
N_EXPERTS = 16
N_EXPERT_GROUPS = 4
EXPERTS_PER_GROUP = N_EXPERTS // N_EXPERT_GROUPS
EXPERT_FF = 512
DEEPNORM_ALPHA = (2 * DEPTH) ** 0.25
LN_EPS = 1e-5
RMS_EPS = 1e-6

LANES = 128
HALO = 8

PAIRS = ((0, 1), (0, 2), (0, 3), (1, 2), (1, 3), (2, 3))
N_CLASSES = N_EXPERT_GROUPS * len(PAIRS)
CLASS_ROWS = 32

C_QKV = 0
C_Z = 768
C_SGU = 1024
C_POOL = 1536
C_MLA = 1792
C_MISC = 2560
IN_PACKED = 2688

ROW_TILE = 512
SEQ_TILE = 512
MOE_TILE = 256
DMA_TILE = 256
ATT_TQ = 256
ATT_TK = 512


def _cparams(sem, vmem_mb=None):
    kw = dict(dimension_semantics=sem)
    if vmem_mb is not None:
        kw["vmem_limit_bytes"] = vmem_mb * 1024 * 1024
    return pltpu.CompilerParams(**kw)


def _layer_norm(x, g, b):
    mu = jnp.mean(x, axis=-1, keepdims=True)
    xc = x - mu
    var = jnp.mean(xc * xc, axis=-1, keepdims=True)
    return xc * lax.rsqrt(var + LN_EPS) * g + b


def _sigmoid(x):
    return 1.0 / (1.0 + jnp.exp(-x))


def _silu(x):
    return x * _sigmoid(x)


def _dot(a, b):
    return jnp.dot(a, b, preferred_element_type=F32)


def _dot_nt(a, b):
    return lax.dot_general(a, b, (((1,), (1,)), ((), ())), preferred_element_type=F32)


def _dot_tn(a, b):
    return lax.dot_general(a, b, (((0,), (0,)), ((), ())), preferred_element_type=F32)


def _in_proj_kernel(pre_ln, x_ref, g_ref, b_ref, w_ref,
                    qkv_ref, z_ref, sgu_ref, pool_ref, mla_ref, misc_ref):
    x = x_ref[...]
    if pre_ln:
        x = _layer_norm(x, g_ref[...], b_ref[...])
    xb = x.astype(BF16)
    qkv_ref[...] = _dot(xb, w_ref[:, C_QKV:C_Z]).astype(BF16)
    z_ref[...] = _dot(xb, w_ref[:, C_Z:C_SGU]).astype(BF16)
    sgu_ref[...] = _dot(xb, w_ref[:, C_SGU:C_POOL]).astype(BF16)
    pool_ref[...] = _dot(xb, w_ref[:, C_POOL:C_MLA]).astype(BF16)
    mla_ref[...] = _dot(xb, w_ref[:, C_MLA:C_MISC]).astype(BF16)
    misc_ref[...] = _dot(xb, w_ref[:, C_MISC:IN_PACKED])


def _in_proj(x, ln_g, ln_b, w_packed, pre_ln):
    t = x.shape[0]
    tm = ROW_TILE
    widths = (768, 256, 512, 256, 768, 128)
    dtypes = (BF16, BF16, BF16, BF16, BF16, F32)
    row = lambda i: (i, 0)
    fixed = lambda i: (0, 0)
    return pl.pallas_call(
        functools.partial(_in_proj_kernel, pre_ln),
        grid=(t // tm,),
        in_specs=[pl.BlockSpec((tm, D_MODEL), row),
                  pl.BlockSpec((1, D_MODEL), fixed),
                  pl.BlockSpec((1, D_MODEL), fixed),
                  pl.BlockSpec((D_MODEL, IN_PACKED), fixed)],
        out_specs=[pl.BlockSpec((tm, w), row) for w in widths],
        out_shape=[jax.ShapeDtypeStruct((t, w), d) for w, d in zip(widths, dtypes)],
        compiler_params=_cparams(("parallel",), 48),
        name="in_proj",
    )(x, ln_g, ln_b, w_packed)


def _pack_in_weights(w):
    d = w.shape[0]
    zeros = lambda n: jnp.zeros((d, n), w.dtype)
    qkv, z = w[:, 0:768], w[:, 768:1024]
    bb, aa = w[:, 1024:1032], w[:, 1032:1040]
    sgu, pool = w[:, 1040:1552], w[:, 1552:1808]
    cq, ckv, kr = w[:, 1808:2192], w[:, 2192:2320], w[:, 2320:2352]
    half = MLA_ROPE // 2
    kr_rot = jnp.concatenate([-kr[:, half:], kr[:, :half]], axis=1)
    rope_tile = jnp.concatenate([zeros(MLA_NOPE), kr, zeros(LANES - MLA_NOPE - MLA_ROPE)], axis=1)
    rot_tile = jnp.concatenate([zeros(MLA_NOPE), kr_rot, zeros(LANES - MLA_NOPE - MLA_ROPE)], axis=1)
    misc = jnp.concatenate([bb, aa, zeros(LANES - 16)], axis=1)
    return jnp.concatenate([qkv, z, sgu, pool, cq, ckv, rope_tile, rot_tile, misc], axis=1).astype(BF16)


def _halo_specs(blk, width, n_seq_blocks):
    per = blk // HALO
    last = n_seq_blocks * per - 1
    main = pl.BlockSpec((None, blk, width), lambda b, i: (b, i, 0))
    prev = pl.BlockSpec((None, HALO, width), lambda b, i: (b, jnp.maximum(i * per - 1, 0), 0))
    nxt = pl.BlockSpec((None, HALO, width), lambda b, i: (b, jnp.minimum((i + 1) * per, last), 0))
    return prev, main, nxt


def _with_halo(prev_ref, main_ref, next_ref):
    i = pl.program_id(1)
    n = pl.num_programs(1)
    prev = jnp.where(i > 0, prev_ref[...].astype(F32), 0.0)
    nxt = jnp.where(i < n - 1, next_ref[...].astype(F32), 0.0)
    return jnp.concatenate([prev, main_ref[...].astype(F32), nxt], axis=0)


def _dn_prep_kernel(prev_ref, main_ref, next_ref, misc_ref, convw_ref, decay_a_ref, dtb_ref, headsum_ref,
                    qkvn_ref, gb_ref):
    blk = main_ref.shape[0]
    xh = _with_halo(prev_ref, main_ref, next_ref)
    acc = jnp.zeros((blk, 3 * GROUP_WIDTH), F32)
    for j in range(DN_CONV):
        off = HALO - DN_CONV // 2 + j
        acc = acc + xh[off:off + blk, :] * convw_ref[j:j + 1, :]
    x = _silu(acc)
    q = x[:, 0:GROUP_WIDTH]
    k = x[:, GROUP_WIDTH:2 * GROUP_WIDTH]
    v = x[:, 2 * GROUP_WIDTH:]
    hs = headsum_ref[...]
    q = q * lax.rsqrt(_dot(q * q, hs) + RMS_EPS) * (DN_DK ** -0.5)
    k = k * lax.rsqrt(_dot(k * k, hs) + RMS_EPS)
    qkvn_ref[:, 0:GROUP_WIDTH] = q.astype(BF16)
    qkvn_ref[:, GROUP_WIDTH:2 * GROUP_WIDTH] = k.astype(BF16)
    qkvn_ref[:, 2 * GROUP_WIDTH:] = v.astype(BF16)
    m = misc_ref[...]
    lane = lax.broadcasted_iota(jnp.int32, m.shape, 1)
    xs = m + dtb_ref[...]
    softplus = jnp.maximum(xs, 0.0) + jnp.log1p(jnp.exp(-jnp.abs(xs)))
    gb = jnp.where(lane < 2 * DN_HEADS, _sigmoid(m), -decay_a_ref[...] * softplus)
    gb_ref[...] = jnp.where(lane < 4 * DN_HEADS, gb, 0.0)


def _dn_prep(qkv, misc, conv_w, a_log, dt_bias, bsz, seq):
    blk = SEQ_TILE
    nb = seq // blk
    qkv3 = qkv.reshape(bsz, seq, 3 * GROUP_WIDTH)
    misc3 = misc.reshape(bsz, seq, LANES)
    prev, main, nxt = _halo_specs(blk, 3 * GROUP_WIDTH, nb)
    convw = jnp.zeros((HALO, 3 * GROUP_WIDTH), F32).at[:DN_CONV].set(conv_w.astype(F32))
    lane_vec = lambda v: jnp.zeros((1, LANES), F32).at[0, 2 * DN_HEADS:4 * DN_HEADS].set(v.reshape(-1).astype(F32))
    decay_a = lane_vec(jnp.exp(a_log.astype(F32)))
    dtb = lane_vec(dt_bias)
    head = np.arange(GROUP_WIDTH) // DN_DK
    headsum = jnp.asarray((head[:, None] == head[None, :]).astype(np.float32))
    fixed = lambda b, i: (0, 0)
    qkvn, gb = pl.pallas_call(
        _dn_prep_kernel,
        grid=(bsz, nb),
        in_specs=[prev, main, nxt,
                  pl.BlockSpec((None, blk, LANES), lambda b, i: (b, i, 0)),
                  pl.BlockSpec((HALO, 3 * GROUP_WIDTH), fixed),
                  pl.BlockSpec((1, LANES), fixed),
                  pl.BlockSpec((1, LANES), fixed),
                  pl.BlockSpec((GROUP_WIDTH, GROUP_WIDTH), fixed)],
        out_specs=[pl.BlockSpec((None, blk, 3 * GROUP_WIDTH), lambda b, i: (b, i, 0)),
                   pl.BlockSpec((None, blk, LANES), lambda b, i: (b, i, 0))],
        out_shape=[jax.ShapeDtypeStruct((bsz, seq, 3 * GROUP_WIDTH), BF16),
                   jax.ShapeDtypeStruct((bsz, seq, LANES), F32)],
        compiler_params=_cparams(("parallel", "parallel")),
        name="dn_prep",
    )(qkv3, qkv3, qkv3, misc3, convw, decay_a, dtb, headsum)
    return qkvn, gb


def _unit_triangular_inverse(a, rows, cols, lower):
    c = a.shape[0]
    hi, lo = (rows, cols) if lower else (cols, rows)
    eye = (rows == cols).astype(F32)
    t = eye
    s = 1
    while s < c:
        sel = ((hi // s) == (lo // s) + 1) & ((hi // (2 * s)) == (lo // (2 * s)))
        cs = jnp.where(sel, a, 0.0)
        if s == 1:
            t = eye - cs
        else:
            t = t - _dot(_dot(t, cs), t)
        s *= 2
    return t


def _dn_main_kernel(qkv_f_ref, gb_f_ref, qkv_b_ref, gb_b_ref, o_f_ref, o_b_ref, state_ref):
    blk = qkv_f_ref.shape[0]
    nc = blk // DN_CHUNK
    c_ = DN_CHUNK

    @pl.when(pl.program_id(1) == 0)
    def _():
        state_ref[...] = jnp.zeros_like(state_ref)

    rows = lax.broadcasted_iota(jnp.int32, (c_, c_), 0)
    cols = lax.broadcasted_iota(jnp.int32, (c_, c_), 1)
    tri_lo = (rows >= cols).astype(F32)
    tri_up = (rows <= cols).astype(F32)

    def chunk_step(ci, carry):
        for d in range(2):
            lower = d == 0
            qkv_ref, gb_ref, o_ref = (qkv_f_ref, gb_f_ref, o_f_ref) if lower else (qkv_b_ref, gb_b_ref, o_b_ref)
            r0 = pl.multiple_of((ci if lower else nc - 1 - ci) * c_, c_)
            x = qkv_ref[pl.ds(r0, c_), :].astype(F32)
            gbt = gb_ref[pl.ds(r0, c_), :]
            incl = (rows >= cols) if lower else (rows <= cols)
            strict = (rows > cols) if lower else (rows < cols)
            gc = jnp.dot(tri_lo if lower else tri_up, gbt, precision=lax.Precision.HIGHEST,
                         preferred_element_type=F32)
            gct = jnp.concatenate([gc, jnp.zeros((LANES - c_, LANES), F32)], axis=0).T[:, :c_]
            g_last = gc[c_ - 1:c_, :] if lower else gc[0:1, :]
            e_gc = jnp.exp(gc)
            e_rest = jnp.exp(g_last - gc)
            e_tot = jnp.exp(g_last)
            outs = []
            for h in range(DN_HEADS):
                lb = DN_HEADS * d + h
                lg = 2 * DN_HEADS + lb
                q = x[:, h * DN_DK:(h + 1) * DN_DK]
                k = x[:, GROUP_WIDTH + h * DN_DK:GROUP_WIDTH + (h + 1) * DN_DK]
                v = x[:, 2 * GROUP_WIDTH + h * DN_DK:2 * GROUP_WIDTH + (h + 1) * DN_DK]
                beta = gbt[:, lb:lb + 1]
                diff = gc[:, lg:lg + 1] - gct[lg:lg + 1, :]
                gam = jnp.where(incl, jnp.exp(jnp.where(incl, diff, 0.0)), 0.0)
                kb = k * beta
                a = jnp.where(strict, _dot_nt(kb, k) * gam, 0.0)
                qk = _dot_nt(q, k) * gam
                t = _unit_triangular_inverse(a, rows, cols, lower)
                rhs = jnp.concatenate([v * beta, kb * e_gc[:, lg:lg + 1]], axis=1)
                uw = _dot(t, rhs)
                u, w = uw[:, :DN_DK], uw[:, DN_DK:]
                s = state_ref[lb]
                qw = jnp.concatenate([q * e_gc[:, lg:lg + 1], w], axis=0)
                qws = _dot(qw, s)
                v_new = u - qws[c_:, :]
                outs.append(qws[:c_, :] + _dot(qk, v_new))
                kd = k * e_rest[:, lg:lg + 1]
                state_ref[lb] = s * e_tot[:, lg:lg + 1] + _dot_tn(kd, v_new)
            o_ref[pl.ds(r0, c_), :] = jnp.concatenate(outs, axis=1)
        return carry

    lax.fori_loop(0, nc, chunk_step, 0)


def _dn_main(qkvn, gb, bsz, seq):
    blk = SEQ_TILE
    nb = seq // blk
    fwd = lambda b, i: (b, i, 0)
    bwd = lambda b, i: (b, nb - 1 - i, 0)
    return pl.pallas_call(
        _dn_main_kernel,
        grid=(bsz, nb),
        in_specs=[pl.BlockSpec((None, blk, 3 * GROUP_WIDTH), fwd),
                  pl.BlockSpec((None, blk, LANES), fwd),
                  pl.BlockSpec((None, blk, 3 * GROUP_WIDTH), bwd),
                  pl.BlockSpec((None, blk, LANES), bwd)],
        out_specs=[pl.BlockSpec((None, blk, GROUP_WIDTH), fwd),
                   pl.BlockSpec((None, blk, GROUP_WIDTH), bwd)],
        out_shape=[jax.ShapeDtypeStruct((bsz, seq, GROUP_WIDTH), F32)] * 2,
        scratch_shapes=[pltpu.VMEM((2 * DN_HEADS, DN_DK, DN_DK), F32)],
        compiler_params=_cparams(("parallel", "arbitrary")),
        name="dn_main",
    )(qkvn, gb, qkvn, gb)


def _sgu_kernel(x_ref, g_ref, b_ref, w_ref, bias_ref, o_ref):
    blk = x_ref.shape[0]
    x = x_ref[...].astype(F32)
    x = 0.5 * x * (1.0 + jnp.tanh(np.sqrt(2.0 / np.pi) * (x + 0.044715 * x * x * x)))
    u = x[:, :SGU_WIDTH]
    v = _layer_norm(x[:, SGU_WIDTH:], g_ref[...], b_ref[...]).astype(BF16)
    group = lax.broadcasted_iota(jnp.int32, (SGU_CHUNK, SGU_WIDTH), 1) // (SGU_WIDTH // SGU_GROUPS)
    for c in range(blk // SGU_CHUNK):
        r = slice(c * SGU_CHUNK, (c + 1) * SGU_CHUNK)
        vc = v[r, :]
        mixed = bias_ref[...]
        for gi in range(SGU_GROUPS):
            mixed = mixed + jnp.where(group == gi, _dot(w_ref[gi], vc), 0.0)
        o_ref[r, :] = (u[r, :] * mixed).astype(BF16)


def _sgu(x, ln_g, ln_b, w_s, b_s, bsz, seq):
    t = x.shape[0]
    blk = SEQ_TILE
    bias = jnp.repeat(b_s.T.astype(F32), SGU_WIDTH // SGU_GROUPS, axis=1)
    fixed2 = lambda i: (0, 0)
    return pl.pallas_call(
        _sgu_kernel,
        grid=(t // blk,),
        in_specs=[pl.BlockSpec((blk, 2 * SGU_WIDTH), lambda i: (i, 0)),
                  pl.BlockSpec((1, SGU_WIDTH), fixed2),
                  pl.BlockSpec((1, SGU_WIDTH), fixed2),
                  pl.BlockSpec((SGU_GROUPS, SGU_CHUNK, SGU_CHUNK), lambda i: (0, 0, 0)),
                  pl.BlockSpec((SGU_CHUNK, SGU_WIDTH), fixed2)],
        out_specs=pl.BlockSpec((blk, SGU_WIDTH), lambda i: (i, 0)),
        out_shape=jax.ShapeDtypeStruct((t, SGU_WIDTH), BF16),
        compiler_params=_cparams(("parallel",)),
        name="sgu",
    )(x, ln_g.reshape(1, -1).astype(F32), ln_b.reshape(1, -1).astype(F32), w_s.astype(BF16), bias)


def _pool_kernel(seq, prev_ref, main_ref, next_ref, w_ref, scale_ref, o_ref):
    blk = main_ref.shape[0]
    n = blk + 2 * HALO
    xh = _with_halo(prev_ref, main_ref, next_ref)
    shift = lambda a, k: pltpu.roll(a, k % n, 0)
    s2 = shift(xh, 1) + xh
    s4 = shift(s2, 1) + shift(s2, -1)
    s8 = shift(s4, 2) + shift(s4, -2)
    s16 = shift(s8, 4) + shift(s8, -4)
    sums = (s2, s4, s8, s16)
    x = xh[HALO:HALO + blk, :]
    group = lax.broadcasted_iota(jnp.int32, (blk, GROUP_WIDTH), 1) // POOL_GW
    pos = lax.broadcasted_iota(jnp.int32, (blk, GROUP_WIDTH), 0) + pl.program_id(1) * blk
    wsum = jnp.zeros((blk, GROUP_WIDTH), F32)
    cnt = jnp.ones((blk, GROUP_WIDTH), F32)
    for gi, win in enumerate(POOL_WINDOWS):
        hi = jnp.minimum(pos + win // 2, seq)
        lo = jnp.maximum(pos - win // 2, 0)
        wsum = jnp.where(group == gi, sums[gi][HALO:HALO + blk, :], wsum)
        cnt = jnp.where(group == gi, (hi - lo).astype(F32), cnt)
    pooled = wsum / cnt - x
    o_ref[...] = (_dot(pooled, w_ref[...]) * scale_ref[...]).astype(BF16)


def _pool(x, w_pool, scale, bsz, seq):
    blk = SEQ_TILE
    nb = seq // blk
    x3 = x.reshape(bsz, seq, GROUP_WIDTH)
    prev, main, nxt = _halo_specs(blk, GROUP_WIDTH, nb)
    wbd = jax.scipy.linalg.block_diag(*[w_pool[g].astype(F32) for g in range(len(POOL_WINDOWS))])
    fixed = lambda b, i: (0, 0)
    out = pl.pallas_call(
        functools.partial(_pool_kernel, seq),
        grid=(bsz, nb),
        in_specs=[prev, main, nxt,
                  pl.BlockSpec((GROUP_WIDTH, GROUP_WIDTH), fixed),
                  pl.BlockSpec((1, GROUP_WIDTH), fixed)],
        out_specs=pl.BlockSpec((None, blk, GROUP_WIDTH), lambda b, i: (b, i, 0)),
        out_shape=jax.ShapeDtypeStruct((bsz, seq, GROUP_WIDTH), BF16),
        compiler_params=_cparams(("parallel", "parallel")),
        name="pool",
    )(x3, x3, x3, wbd, scale.reshape(1, -1).astype(F32))
    return out.reshape(bsz * seq, GROUP_WIDTH)


def _rms(x, g):
    return x * lax.rsqrt(jnp.mean(x * x, axis=-1, keepdims=True) + RMS_EPS) * g


def _mla_prep_kernel(x_ref, qg_ref, kvg_ref, wq_ref, wkv_ref, cos_ref, sin_ref, q_ref, k_ref, v_ref):
    hw = MLA_HEADS * LANES
    cq = _rms(x_ref[:, 0:MLA_Q_RANK].astype(F32), qg_ref[...]).astype(BF16)
    ckv = _rms(x_ref[:, MLA_Q_RANK:MLA_Q_RANK + MLA_KV_RANK].astype(F32), kvg_ref[...]).astype(BF16)
    c0 = MLA_Q_RANK + MLA_KV_RANK
    cos = cos_ref[...]
    sin = sin_ref[...]
    k_rope = (x_ref[:, c0:c0 + LANES].astype(F32) * cos
              + x_ref[:, c0 + LANES:c0 + 2 * LANES].astype(F32) * sin)
    qa = _dot(cq, wq_ref[:, 0:hw])
    qb = _dot(cq, wq_ref[:, hw:2 * hw])
    kn = _dot(ckv, wkv_ref[:, 0:hw])
    v_ref[...] = _dot(ckv, wkv_ref[:, hw:2 * hw]).astype(BF16)
    for h in range(MLA_HEADS):
        c = slice(h * LANES, (h + 1) * LANES)
        q_ref[:, c] = (qa[:, c] * cos + qb[:, c] * sin).astype(BF16)
        k_ref[:, c] = (kn[:, c] + k_rope).astype(BF16)


def _rope_tables(seq):
    inv = ROPE_THETA ** (-jnp.arange(0, MLA_ROPE, 2, dtype=F32) / MLA_ROPE)
    ang = jnp.arange(seq, dtype=F32)[:, None] * inv[None, :]
    pad = LANES - MLA_NOPE - MLA_ROPE
    cos = jnp.concatenate([jnp.ones((seq, MLA_NOPE), F32), jnp.cos(ang), jnp.cos(ang), jnp.zeros((seq, pad), F32)], 1)
    sin = jnp.concatenate([jnp.zeros((seq, MLA_NOPE), F32), jnp.sin(ang), jnp.sin(ang), jnp.zeros((seq, pad), F32)], 1)
    return cos, sin


def _pack_mla_weights(w_uq, w_uk, w_uv):
    half = MLA_ROPE // 2
    scale = (MLA_NOPE + MLA_ROPE) ** -0.5
    pad = LANES - MLA_NOPE - MLA_ROPE
    qa, qb, kn, vv = [], [], [], []
    for h in range(MLA_HEADS):
        wq = w_uq[:, h, :].astype(F32) * scale
        rope = wq[:, MLA_NOPE:]
        rot = jnp.concatenate([-rope[:, half:], rope[:, :half]], axis=1)
        zq = lambda n: jnp.zeros((MLA_Q_RANK, n), F32)
        zk = lambda n: jnp.zeros((MLA_KV_RANK, n), F32)
        qa.append(jnp.concatenate([wq, zq(pad)], axis=1))
        qb.append(jnp.concatenate([zq(MLA_NOPE), rot, zq(pad)], axis=1))
        kn.append(jnp.concatenate([w_uk[:, h, :].astype(F32), zk(LANES - MLA_NOPE)], axis=1))
        wv = w_uv[:, h, :].astype(F32)
        vv.append(jnp.concatenate([wv, zk(MLA_V)] if h % 2 == 0 else [zk(MLA_V), wv], axis=1))
    wq_packed = jnp.concatenate(qa + qb, axis=1).astype(BF16)
    wkv_packed = jnp.concatenate(kn + vv, axis=1).astype(BF16)
    return wq_packed, wkv_packed


def _mla_prep(x, q_g, kv_g, wq_packed, wkv_packed, cos, sin, seq):
    t = x.shape[0]
    tm = ROW_TILE
    per_seq = seq // tm
    hw = MLA_HEADS * LANES
    row = lambda i: (i, 0)
    fixed = lambda i: (0, 0)
    table = lambda i: (i % per_seq, 0)
    return pl.pallas_call(
        _mla_prep_kernel,
        grid=(t // tm,),
        in_specs=[pl.BlockSpec((tm, 768), row),
                  pl.BlockSpec((1, MLA_Q_RANK), fixed),
                  pl.BlockSpec((1, MLA_KV_RANK), fixed),
                  pl.BlockSpec((MLA_Q_RANK, 2 * hw), fixed),
                  pl.BlockSpec((MLA_KV_RANK, 2 * hw), fixed),
                  pl.BlockSpec((tm, LANES), table),
                  pl.BlockSpec((tm, LANES), table)],
        out_specs=[pl.BlockSpec((tm, hw), row)] * 3,
        out_shape=[jax.ShapeDtypeStruct((t, hw), BF16)] * 3,
        compiler_params=_cparams(("parallel",)),
        name="mla_prep",
    )(x, q_g.reshape(1, -1).astype(F32), kv_g.reshape(1, -1).astype(F32), wq_packed, wkv_packed, cos, sin)


def _attn_kernel(q_ref, k_ref, v_ref, o_ref):
    seq = k_ref.shape[0]
    tq = q_ref.shape[0]
    tk = ATT_TK
    out = jnp.zeros((tq, LANES), F32)
    for h in range(2):
        c = slice(h * LANES, (h + 1) * LANES)
        q = q_ref[:, c]

        def kv_step(j, carry, c=c, q=q):
            m, l, acc = carry
            r0 = pl.multiple_of(j * tk, tk)
            s = _dot_nt(q, k_ref[pl.ds(r0, tk), c])
            m_new = jnp.maximum(m, jnp.max(s, axis=-1, keepdims=True))
            alpha = jnp.exp(m - m_new)
            p = jnp.exp(s - m_new)
            l = l * alpha + jnp.sum(p, axis=-1, keepdims=True)
            acc = acc * alpha + _dot(p.astype(BF16), v_ref[pl.ds(r0, tk), c])
            return m_new, l, acc

        init = (jnp.full((tq, 1), -jnp.inf, F32), jnp.zeros((tq, 1), F32), jnp.zeros((tq, LANES), F32))
        _, l, acc = lax.fori_loop(0, seq // tk, kv_step, init)
        out = out + acc / l
    o_ref[...] = out.astype(BF16)


def _attention(q, k, v, bsz, seq):
    hw = MLA_HEADS * LANES
    q3, k3, v3 = (a.reshape(bsz, seq, hw) for a in (q, k, v))
    tq = ATT_TQ
    out = pl.pallas_call(
        _attn_kernel,
        grid=(bsz, 2, seq // tq),
        in_specs=[pl.BlockSpec((None, tq, 2 * LANES), lambda b, p, i: (b, i, p)),
                  pl.BlockSpec((None, seq, 2 * LANES), lambda b, p, i: (b, 0, p)),
                  pl.BlockSpec((None, seq, 2 * LANES), lambda b, p, i: (b, 0, p))],
        out_specs=pl.BlockSpec((None, tq, LANES), lambda b, p, i: (b, i, p)),
        out_shape=jax.ShapeDtypeStruct((bsz, seq, GROUP_WIDTH), BF16),
        compiler_params=_cparams(("parallel", "parallel", "arbitrary"), 48),
        name="mla_attention",
    )(q3, k3, v3)
    return out.reshape(bsz * seq, GROUP_WIDTH)


def _out_proj_kernel(pre_ln, x_ref, lng_ref, lnb_ref, of_ref, ob_ref, z_ref, ng_ref, headmean_ref,
                     yb_ref, yc_ref, yd_ref, w_ref, g_ref, b_ref, o_ref):
    x = x_ref[...]
    if pre_ln:
        x = _layer_norm(x, lng_ref[...], lnb_ref[...])
    o = of_ref[...] + ob_ref[...]
    z = z_ref[...].astype(F32)
    ya = o * lax.rsqrt(_dot(o * o, headmean_ref[...]) + RMS_EPS) * ng_ref[...] * _silu(z)
    gw = GROUP_WIDTH
    mix = (_dot(ya.astype(BF16), w_ref[0:gw, :]) + _dot(yb_ref[...], w_ref[gw:2 * gw, :])
           + _dot(yc_ref[...], w_ref[2 * gw:3 * gw, :]) + _dot(yd_ref[...], w_ref[3 * gw:, :]))
    o_ref[...] = _layer_norm(DEEPNORM_ALPHA * x + mix, g_ref[...], b_ref[...])


def _out_proj(x, ln_in_g, ln_in_b, o_f, o_b, z, norm_g, y_b, y_c, y_d, w_out, ln_g, ln_b, pre_ln):
    t = x.shape[0]
    tm = ROW_TILE
    head = np.arange(GROUP_WIDTH) // DN_DK
    headmean = jnp.asarray((head[:, None] == head[None, :]).astype(np.float32) / DN_DK)
    ng = jnp.tile(norm_g.astype(F32), DN_HEADS).reshape(1, GROUP_WIDTH)
    row = lambda i: (i, 0)
    fixed = lambda i: (0, 0)
    quarter = pl.BlockSpec((tm, GROUP_WIDTH), row)
    vec = pl.BlockSpec((1, D_MODEL), fixed)
    return pl.pallas_call(
        functools.partial(_out_proj_kernel, pre_ln),
        grid=(t // tm,),
        in_specs=[pl.BlockSpec((tm, D_MODEL), row), vec, vec,
                  quarter, quarter, quarter,
                  pl.BlockSpec((1, GROUP_WIDTH), fixed),
                  pl.BlockSpec((GROUP_WIDTH, GROUP_WIDTH), fixed),
                  quarter, quarter, quarter,
                  pl.BlockSpec((D_MODEL, D_MODEL), fixed), vec, vec],
        out_specs=pl.BlockSpec((tm, D_MODEL), row),
        out_shape=jax.ShapeDtypeStruct((t, D_MODEL), F32),
        compiler_params=_cparams(("parallel",)),
        name="out_proj",
    )(x, ln_in_g, ln_in_b, o_f, o_b, z, ng, headmean, y_b, y_c, y_d, w_out.astype(BF16),
      ln_g.reshape(1, -1).astype(F32), ln_b.reshape(1, -1).astype(F32))


def _router_kernel(x_ref, w_ref, bias_ref, cls_ref, rank_ref, cnt_ref, carry_ref):
    tm = x_ref.shape[0]

    @pl.when(pl.program_id(0) == 0)
    def _():
        carry_ref[...] = jnp.zeros_like(carry_ref)

    logits = jnp.dot(x_ref[...], w_ref[...], precision=lax.Precision.HIGHEST, preferred_element_type=F32)
    lane = lax.broadcasted_iota(jnp.int32, (tm, LANES), 1)
    valid = lane < N_EXPERTS
    biased = jnp.where(valid, _sigmoid(logits) + bias_ref[...], -jnp.inf)
    member = lane % EXPERTS_PER_GROUP
    neg = jnp.full((tm, LANES), -jnp.inf, F32)
    best_pair = neg
    beaten = jnp.zeros((tm, LANES), jnp.int32)
    others = []
    for k in range(1, EXPERTS_PER_GROUP):
        wrapped = member + k >= EXPERTS_PER_GROUP
        other = jnp.where(wrapped, pltpu.roll(biased, EXPERTS_PER_GROUP - k, 1),
                          pltpu.roll(biased, LANES - k, 1))
        others.append(other)
        best_pair = jnp.maximum(best_pair, biased + other)
        beaten = beaten + ((other > biased) | ((other == biased) & wrapped)).astype(jnp.int32)
    best_pair = jnp.maximum(best_pair, jnp.maximum(others[0] + others[1],
                                                   jnp.maximum(others[0] + others[2], others[1] + others[2])))
    group_score = jnp.where(valid, best_pair, -jnp.inf)
    top = jnp.max(group_score, axis=-1, keepdims=True)
    group = lane // EXPERTS_PER_GROUP
    chosen = jnp.min(jnp.where(group_score == top, group, N_EXPERT_GROUPS), axis=-1, keepdims=True)
    picked = valid & (group == chosen) & (beaten < 2)
    e_lo = jnp.min(jnp.where(picked, lane, LANES), axis=-1, keepdims=True)
    e_hi = jnp.max(jnp.where(picked, lane, -1), axis=-1, keepdims=True)
    i = e_lo % EXPERTS_PER_GROUP
    j = e_hi % EXPERTS_PER_GROUP
    cls = chosen * len(PAIRS) + (i * (7 - i)) // 2 + (j - i - 1)
    cls_t = jnp.broadcast_to(cls.astype(F32), (tm, LANES)).T
    cls_row = cls_t[0:1, :]
    onehot = (lax.broadcasted_iota(jnp.int32, (CLASS_ROWS, tm), 0).astype(F32) == cls_row)
    upto = (lax.broadcasted_iota(jnp.int32, (tm, tm), 0) <= lax.broadcasted_iota(jnp.int32, (tm, tm), 1))
    prefix = _dot(onehot.astype(BF16), upto.astype(BF16))
    carry = carry_ref[:, 0:1]
    rank = jnp.sum(jnp.where(onehot, prefix + carry, 0.0), axis=0, keepdims=True) - 1.0
    cls_ref[...] = cls_t[0:HALO, :]
    rank_ref[...] = jnp.broadcast_to(rank, (HALO, tm))
    total = carry + prefix[:, tm - 1:tm]
    carry_ref[...] = jnp.broadcast_to(total, carry_ref.shape)
    cnt_ref[...] = jnp.broadcast_to(total, cnt_ref.shape)


def _router(x, router_w, router_bias):
    t = x.shape[0]
    tm = ROW_TILE
    w = jnp.zeros((D_MODEL, LANES), F32).at[:, :N_EXPERTS].set(router_w.astype(F32))
    bias = jnp.zeros((1, LANES), F32).at[0, :N_EXPERTS].set(router_bias.astype(F32))
    fixed = lambda i: (0, 0)
    cls, rank, cnt = pl.pallas_call(
        _router_kernel,
        grid=(t // tm,),
        in_specs=[pl.BlockSpec((tm, D_MODEL), lambda i: (i, 0)),
                  pl.BlockSpec((D_MODEL, LANES), fixed),
                  pl.BlockSpec((1, LANES), fixed)],
        out_specs=[pl.BlockSpec((HALO, tm), lambda i: (0, i)),
                   pl.BlockSpec((HALO, tm), lambda i: (0, i)),
                   pl.BlockSpec((CLASS_ROWS, LANES), fixed)],
        out_shape=[jax.ShapeDtypeStruct((HALO, t), F32),
                   jax.ShapeDtypeStruct((HALO, t), F32),
                   jax.ShapeDtypeStruct((CLASS_ROWS, LANES), F32)],
        scratch_shapes=[pltpu.VMEM((CLASS_ROWS, LANES), F32)],
        compiler_params=_cparams(("arbitrary",)),
        name="router",
    )(x, w, bias)
    return cls[0].astype(jnp.int32), rank[0].astype(jnp.int32), cnt[:N_CLASSES, 0].astype(jnp.int32)


def _routing_tables(cls, rank, counts, n_tiles):
    tm = MOE_TILE
    tiles = (counts + tm - 1) // tm
    first_tile = jnp.cumsum(tiles) - tiles
    dest = first_tile[cls] * tm + rank
    tile_ids = jnp.arange(n_tiles, dtype=jnp.int32)
    tile_cls = jnp.clip(jnp.searchsorted(jnp.cumsum(tiles), tile_ids, side="right"), 0, N_CLASSES - 1)
    used = (tile_ids < jnp.sum(tiles)).astype(jnp.int32)
    pair = np.asarray(PAIRS, np.int32)
    base = (tile_cls // len(PAIRS)) * EXPERTS_PER_GROUP
    e1 = base + jnp.asarray(pair[:, 0])[tile_cls % len(PAIRS)]
    e2 = base + jnp.asarray(pair[:, 1])[tile_cls % len(PAIRS)]
    return dest.astype(jnp.int32), e1.astype(jnp.int32), e2.astype(jnp.int32), used


def _dispatch_kernel(dest_ref, x_ref, zeros_ref, xs_ref, sem):
    del zeros_ref
    tr = x_ref.shape[0]

    def row_copy(r):
        return pltpu.make_async_copy(x_ref.at[pl.ds(r, 1), :], xs_ref.at[pl.ds(dest_ref[0, 0, r], 1), :], sem)

    def start(r, c):
        row_copy(r).start()
        return c

    def wait(r, c):
        row_copy(r).wait()
        return c

    lax.fori_loop(0, tr, start, 0)
    lax.fori_loop(0, tr, wait, 0)


def _dispatch(x, dest, n_slots):
    t = x.shape[0]
    tr = DMA_TILE
    dest3 = dest.reshape(t // tr, 1, tr)
    zeros = jnp.zeros((n_slots, D_MODEL), F32)
    return pl.pallas_call(
        _dispatch_kernel,
        grid=(t // tr,),
        in_specs=[pl.BlockSpec((1, 1, tr), lambda i: (i, 0, 0), memory_space=pltpu.SMEM),
                  pl.BlockSpec((tr, D_MODEL), lambda i: (i, 0)),
                  pl.BlockSpec(memory_space=pl.ANY)],
        out_specs=pl.BlockSpec(memory_space=pl.ANY),
        out_shape=jax.ShapeDtypeStruct((n_slots, D_MODEL), F32),
        scratch_shapes=[pltpu.SemaphoreType.DMA],
        input_output_aliases={2: 0},
        compiler_params=_cparams(("arbitrary",)),
        name="moe_dispatch",
    )(dest3, x, zeros)


def _moe_kernel(e1_ref, e2_ref, used_ref, x_ref, rw_ref, wg1_ref, wu1_ref, wd1_ref, wg2_ref, wu2_ref, wd2_ref, y_ref):
    i = pl.program_id(0)

    @pl.when(used_ref[i] == 1)
    def _():
        xb = x_ref[...].astype(BF16)
        scores = _sigmoid(_dot(xb, rw_ref[...]))
        lane = lax.broadcasted_iota(jnp.int32, scores.shape, 1)
        s1 = jnp.sum(jnp.where(lane == e1_ref[i], scores, 0.0), axis=-1, keepdims=True)
        s2 = jnp.sum(jnp.where(lane == e2_ref[i], scores, 0.0), axis=-1, keepdims=True)
        inv = 1.0 / (s1 + s2)
        h1 = _silu(_dot(xb, wg1_ref[...])) * _dot(xb, wu1_ref[...]) * (s1 * inv)
        h2 = _silu(_dot(xb, wg2_ref[...])) * _dot(xb, wu2_ref[...]) * (s2 * inv)
        y_ref[...] = _dot(h1.astype(BF16), wd1_ref[...]) + _dot(h2.astype(BF16), wd2_ref[...])

    @pl.when(used_ref[i] == 0)
    def _():
        y_ref[...] = jnp.zeros_like(y_ref)


def _moe(xs, e1, e2, used, router_w, w_gate, w_up, w_down):
    n_slots = xs.shape[0]
    tm = MOE_TILE
    rw = jnp.zeros((D_MODEL, LANES), BF16).at[:, :N_EXPERTS].set(router_w.astype(BF16))
    up = lambda sel: pl.BlockSpec((None, D_MODEL, EXPERT_FF), lambda i, e1, e2, u: ((e1, e2)[sel][i], 0, 0))
    down = lambda sel: pl.BlockSpec((None, EXPERT_FF, D_MODEL), lambda i, e1, e2, u: ((e1, e2)[sel][i], 0, 0))
    grid_spec = pltpu.PrefetchScalarGridSpec(
        num_scalar_prefetch=3,
        grid=(n_slots // tm,),
        in_specs=[pl.BlockSpec((tm, D_MODEL), lambda i, e1, e2, u: (i, 0)),
                  pl.BlockSpec((D_MODEL, LANES), lambda i, e1, e2, u: (0, 0)),
                  up(0), up(0), down(0), up(1), up(1), down(1)],
        out_specs=pl.BlockSpec((tm, D_MODEL), lambda i, e1, e2, u: (i, 0)),
    )
    return pl.pallas_call(
        _moe_kernel,
        grid_spec=grid_spec,
        out_shape=jax.ShapeDtypeStruct((n_slots, D_MODEL), F32),
        compiler_params=_cparams(("arbitrary",), 48),
        name="moe_experts",
    )(e1, e2, used, xs, rw, w_gate, w_up, w_down, w_gate, w_up, w_down)


def _combine_kernel(dest_ref, x_ref, ys_ref, g_ref, b_ref, o_ref, buf, sem):
    tr = x_ref.shape[0]

    def row_copy(r):
        return pltpu.make_async_copy(ys_ref.at[pl.ds(dest_ref[0, 0, r], 1), :], buf.at[pl.ds(r, 1), :], sem)

    def start(r, c):
        row_copy(r).start()
        return c

    def wait(r, c):
        row_copy(r).wait()
        return c

    lax.fori_loop(0, tr, start, 0)
    lax.fori_loop(0, tr, wait, 0)
    o_ref[...] = _layer_norm(DEEPNORM_ALPHA * x_ref[...] + buf[...], g_ref[...], b_ref[...])


def _combine(x, ys, dest, ln_g, ln_b):
    t = x.shape[0]
    tr = DMA_TILE
    dest3 = dest.reshape(t // tr, 1, tr)
    fixed = lambda i: (0, 0)
    return pl.pallas_call(
        _combine_kernel,
        grid=(t // tr,),
        in_specs=[pl.BlockSpec((1, 1, tr), lambda i: (i, 0, 0), memory_space=pltpu.SMEM),
                  pl.BlockSpec((tr, D_MODEL), lambda i: (i, 0)),
                  pl.BlockSpec(memory_space=pl.ANY),
                  pl.BlockSpec((1, D_MODEL), fixed),
                  pl.BlockSpec((1, D_MODEL), fixed)],
        out_specs=pl.BlockSpec((tr, D_MODEL), lambda i: (i, 0)),
        out_shape=jax.ShapeDtypeStruct((t, D_MODEL), F32),
        scratch_shapes=[pltpu.VMEM((tr, D_MODEL), F32), pltpu.SemaphoreType.DMA],
        compiler_params=_cparams(("arbitrary",)),
        name="moe_combine",
    )(dest3, x, ys, ln_g.reshape(1, -1).astype(F32), ln_b.reshape(1, -1).astype(F32))


def _trunk(x, p):
    bsz, seq, d = x.shape
    t = bsz * seq
    x = x.reshape(t, d)
    cos, sin = _rope_tables(seq)
    n_tiles = t // MOE_TILE + N_CLASSES
    for l in range(DEPTH):
        first = l == 0
        qkv, z, sgu_in, pool_in, mla_in, misc = _in_proj(x, p["ln_in_g"], p["ln_in_b"], p["w_in"][l], first)
        qkvn, gb = _dn_prep(qkv, misc, p["dn_conv_w"][l], p["dn_a_log"][l], p["dn_dt_bias"][l], bsz, seq)
        o_f, o_b = _dn_main(qkvn, gb, bsz, seq)
        y_b = _sgu(sgu_in, p["sgu_ln_g"][l], p["sgu_ln_b"][l], p["sgu_w"][l], p["sgu_b"][l], bsz, seq)
        y_c = _pool(pool_in, p["pool_w"][l], p["pool_scale"][l], bsz, seq)
        q, k, v = _mla_prep(mla_in, p["mla_q_norm_g"][l], p["mla_kv_norm_g"][l], p["wq"][l], p["wkv"][l],
                            cos, sin, seq)
        y_d = _attention(q, k, v, bsz, seq)
        x = _out_proj(x, p["ln_in_g"], p["ln_in_b"], o_f.reshape(t, -1), o_b.reshape(t, -1), z,
                      p["dn_norm_g"][l], y_b, y_c, y_d, p["w_out"][l], p["ln1_g"][l], p["ln1_b"][l], first)
        cls, rank, counts = _router(x, p["router_w"], p["router_bias"])
        dest, e1, e2, used = _routing_tables(cls, rank, counts, n_tiles)
        xs = _dispatch(x, dest, n_tiles * MOE_TILE)
        ys = _moe(xs, e1, e2, used, p["router_w"], p["moe_w_gate"][l], p["moe_w_up"][l], p["moe_w_down"][l])
        x = _combine(x, ys, dest, p["ln2_g"][l], p["ln2_b"][l])
    return x.reshape(bsz, seq, d)


def kernel(x_prompt, x_sample, ln_in_g, ln_in_b, w_in, dn_conv_w, dn_a_log, dn_dt_bias, dn_norm_g, sgu_ln_g, sgu_ln_b, sgu_w, sgu_b, pool_w, pool_scale, mla_q_norm_g, mla_kv_norm_g, mla_w_uq, mla_w_uk, mla_w_uv, w_out, ln1_g, ln1_b, router_w, router_bias, moe_w_gate, moe_w_up, moe_w_down, ln2_g, ln2_b):
    packed_mla = [_pack_mla_weights(mla_w_uq[l], mla_w_uk[l], mla_w_uv[l]) for l in range(DEPTH)]
    p = dict(
        ln_in_g=ln_in_g.reshape(1, -1).astype(F32), ln_in_b=ln_in_b.reshape(1, -1).astype(F32),
        w_in=[_pack_in_weights(w_in[l]) for l in range(DEPTH)],
        dn_conv_w=dn_conv_w, dn_a_log=dn_a_log, dn_dt_bias=dn_dt_bias, dn_norm_g=dn_norm_g,
        sgu_ln_g=sgu_ln_g, sgu_ln_b=sgu_ln_b, sgu_w=sgu_w, sgu_b=sgu_b,
        pool_w=pool_w, pool_scale=pool_scale,
        mla_q_norm_g=mla_q_norm_g, mla_kv_norm_g=mla_kv_norm_g,
        wq=[w[0] for w in packed_mla], wkv=[w[1] for w in packed_mla],
        w_out=w_out, ln1_g=ln1_g, ln1_b=ln1_b, router_w=router_w, router_bias=router_bias,
        moe_w_gate=moe_w_gate.astype(BF16), moe_w_up=moe_w_up.astype(BF16), moe_w_down=moe_w_down.astype(BF16),
        ln2_g=ln2_g, ln2_b=ln2_b,
    )
    return _trunk(x_prompt, p), _trunk(x_sample, p)
```

```python
import functools

import numpy as np
import jax
import jax.numpy as jnp
from jax import lax
from jax.experimental import pallas as pl
from jax.experimental.pallas import tpu as pltpu

F32 = jnp.float32
BF16 = jnp.bfloat16

D_MODEL = 1024
DEPTH = 2
GROUP_WIDTH = D_MODEL // 4
DN_HEADS = 4
DN_DK = GROUP_WIDTH // DN_HEADS
DN_CONV = 5
DN_CHUNK = 64
DN_GROUP = 4
SGU_GROUPS = 4
SGU_CHUNK = 128
SGU_WIDTH = GROUP_WIDTH
POOL_WINDOWS = (2, 4, 8, 16)
POOL_GW = GROUP_WIDTH // len(POOL_WINDOWS)
MLA_HEADS = 4
MLA_NOPE = 64
MLA_ROPE = 32
MLA_V = GROUP_WIDTH // MLA_HEADS
MLA_VROWS = MLA_V + 16
MLA_Q_RANK = 384
MLA_KV_RANK = 128
ROPE_THETA = 10000.0
N_EXPERTS = 16
N_EXPERT_GROUPS = 4
EXPERTS_PER_GROUP = N_EXPERTS // N_EXPERT_GROUPS
EXPERT_FF = 512
DEEPNORM_ALPHA = (2 * DEPTH) ** 0.25
LN_EPS = 1e-5
RMS_EPS = 1e-6

LANES = 128
HALO = 8

PAIRS = ((0, 1), (0, 2), (0, 3), (1, 2), (1, 3), (2, 3))
N_CLASSES = N_EXPERT_GROUPS * len(PAIRS)
CLASS_ROWS = 32

C_QKV = 0
C_Z = 768
C_SGU = 1024
C_POOL = 1536
C_MLA = 1792
C_MISC = 2560
IN_PACKED = 2688

ROW_TILE = 512
SEQ_TILE = 512
MOE_TILE = 256
DMA_TILE = 512
ATT_TQ = 256
ATT_TK = 256
ATT_UNROLL = 8
ATT_LOOKAHEAD = 3


def _cparams(sem, vmem_mb=None):
    kw = dict(dimension_semantics=sem)
    if vmem_mb is not None:
        kw["vmem_limit_bytes"] = vmem_mb * 1024 * 1024
    return pltpu.CompilerParams(**kw)


def _layer_norm(x, g, b):
    mu = jnp.mean(x, axis=-1, keepdims=True)
    xc = x - mu
    var = jnp.mean(xc * xc, axis=-1, keepdims=True)
    return xc * lax.rsqrt(var + LN_EPS) * g + b


def _sigmoid(x):
    return 1.0 / (1.0 + jnp.exp(-x))


def _silu(x):
    return x * _sigmoid(x)


def _dot(a, b):
    return jnp.dot(a, b, preferred_element_type=F32)


def _dot_nt(a, b):
    return lax.dot_general(a, b, (((1,), (1,)), ((), ())), preferred_element_type=F32)


def _dot_tn(a, b):
    return lax.dot_general(a, b, (((0,), (0,)), ((), ())), preferred_element_type=F32)


def _in_proj_kernel(pre_ln, x_ref, g_ref, b_ref, w_ref,
                    qkv_ref, z_ref, sgu_ref, pool_ref, mla_ref, misc_ref):
    x = x_ref[...]
    if pre_ln:
        x = _layer_norm(x, g_ref[...], b_ref[...])
    xb = x.astype(BF16)
    qkv_ref[...] = _dot(xb, w_ref[:, C_QKV:C_Z]).astype(BF16)
    z_ref[...] = _dot(xb, w_ref[:, C_Z:C_SGU]).astype(BF16)
    sgu_ref[...] = _dot(xb, w_ref[:, C_SGU:C_POOL]).astype(BF16)
    pool_ref[...] = _dot(xb, w_ref[:, C_POOL:C_MLA]).astype(BF16)
    mla_ref[...] = _dot(xb, w_ref[:, C_MLA:C_MISC]).astype(BF16)
    misc_ref[...] = _dot(xb, w_ref[:, C_MISC:IN_PACKED])


def _in_proj(x, ln_g, ln_b, w_packed, pre_ln):
    t = x.shape[0]
    tm = ROW_TILE
    widths = (768, 256, 512, 256, 768, 128)
    dtypes = (BF16, BF16, BF16, BF16, BF16, F32)
    row = lambda i: (i, 0)
    fixed = lambda i: (0, 0)
    return pl.pallas_call(
        functools.partial(_in_proj_kernel, pre_ln),
        grid=(t // tm,),
        in_specs=[pl.BlockSpec((tm, D_MODEL), row),
                  pl.BlockSpec((1, D_MODEL), fixed),
                  pl.BlockSpec((1, D_MODEL), fixed),
                  pl.BlockSpec((D_MODEL, IN_PACKED), fixed)],
        out_specs=[pl.BlockSpec((tm, w), row) for w in widths],
        out_shape=[jax.ShapeDtypeStruct((t, w), d) for w, d in zip(widths, dtypes)],
        compiler_params=_cparams(("parallel",), 48),
        name="in_proj",
    )(x, ln_g, ln_b, w_packed)


def _pack_in_weights(w):
    d = w.shape[0]
    zeros = lambda n: jnp.zeros((d, n), w.dtype)
    qkv, z = w[:, 0:768], w[:, 768:1024]
    bb, aa = w[:, 1024:1032], w[:, 1032:1040]
    sgu, pool = w[:, 1040:1552], w[:, 1552:1808]
    cq, ckv, kr = w[:, 1808:2192], w[:, 2192:2320], w[:, 2320:2352]
    half = MLA_ROPE // 2
    kr_rot = jnp.concatenate([-kr[:, half:], kr[:, :half]], axis=1)
    rope_tile = jnp.concatenate([zeros(MLA_NOPE), kr, zeros(LANES - MLA_NOPE - MLA_ROPE)], axis=1)
    rot_tile = jnp.concatenate([zeros(MLA_NOPE), kr_rot, zeros(LANES - MLA_NOPE - MLA_ROPE)], axis=1)
    misc = jnp.concatenate([bb, aa, zeros(LANES - 16)], axis=1)
    return jnp.concatenate([qkv, z, sgu, pool, cq, ckv, rope_tile, rot_tile, misc], axis=1).astype(BF16)


def _halo_specs(blk, width, n_seq_blocks):
    per = blk // HALO
    last = n_seq_blocks * per - 1
    main = pl.BlockSpec((None, blk, width), lambda b, i: (b, i, 0))
    prev = pl.BlockSpec((None, HALO, width), lambda b, i: (b, jnp.maximum(i * per - 1, 0), 0))
    nxt = pl.BlockSpec((None, HALO, width), lambda b, i: (b, jnp.minimum((i + 1) * per, last), 0))
    return prev, main, nxt


def _with_halo(prev_ref, main_ref, next_ref):
    i = pl.program_id(1)
    n = pl.num_programs(1)
    prev = jnp.where(i > 0, prev_ref[...].astype(F32), 0.0)
    nxt = jnp.where(i < n - 1, next_ref[...].astype(F32), 0.0)
    return jnp.concatenate([prev, main_ref[...].astype(F32), nxt], axis=0)


def _dn_prep_kernel(prev_ref, main_ref, next_ref, misc_ref, convw_ref, decay_a_ref, dtb_ref, headsum_ref,
                    qkvn_ref, gb_ref):
    blk = main_ref.shape[0]
    xh = _with_halo(prev_ref, main_ref, next_ref)
    acc = jnp.zeros((blk, 3 * GROUP_WIDTH), F32)
    for j in range(DN_CONV):
        off = HALO - DN_CONV // 2 + j
        acc = acc + xh[off:off + blk, :] * convw_ref[j:j + 1, :]
    x = _silu(acc)
    q = x[:, 0:GROUP_WIDTH]
    k = x[:, GROUP_WIDTH:2 * GROUP_WIDTH]
    v = x[:, 2 * GROUP_WIDTH:]
    hs = headsum_ref[...]
    q = q * lax.rsqrt(_dot(q * q, hs) + RMS_EPS) * (DN_DK ** -0.5)
    k = k * lax.rsqrt(_dot(k * k, hs) + RMS_EPS)
    qkvn_ref[:, 0:GROUP_WIDTH] = q.astype(BF16)
    qkvn_ref[:, GROUP_WIDTH:2 * GROUP_WIDTH] = k.astype(BF16)
    qkvn_ref[:, 2 * GROUP_WIDTH:] = v.astype(BF16)
    m = misc_ref[...]
    lane = lax.broadcasted_iota(jnp.int32, m.shape, 1)
    xs = m + dtb_ref[...]
    softplus = jnp.maximum(xs, 0.0) + jnp.log1p(jnp.exp(-jnp.abs(xs)))
    gb = jnp.where(lane < 2 * DN_HEADS, _sigmoid(m), -decay_a_ref[...] * softplus)
    gb_ref[...] = jnp.where(lane < 4 * DN_HEADS, gb, 0.0)


def _dn_prep(qkv, misc, conv_w, a_log, dt_bias, bsz, seq):
    blk = SEQ_TILE
    nb = seq // blk
    qkv3 = qkv.reshape(bsz, seq, 3 * GROUP_WIDTH)
    misc3 = misc.reshape(bsz, seq, LANES)
    prev, main, nxt = _halo_specs(blk, 3 * GROUP_WIDTH, nb)
    convw = jnp.zeros((HALO, 3 * GROUP_WIDTH), F32).at[:DN_CONV].set(conv_w.astype(F32))
    lane_vec = lambda v: jnp.zeros((1, LANES), F32).at[0, 2 * DN_HEADS:4 * DN_HEADS].set(v.reshape(-1).astype(F32))
    decay_a = lane_vec(jnp.exp(a_log.astype(F32)))
    dtb = lane_vec(dt_bias)
    head = np.arange(GROUP_WIDTH) // DN_DK
    headsum = jnp.asarray((head[:, None] == head[None, :]).astype(np.float32))
    fixed = lambda b, i: (0, 0)
    qkvn, gb = pl.pallas_call(
        _dn_prep_kernel,
        grid=(bsz, nb),
        in_specs=[prev, main, nxt,
                  pl.BlockSpec((None, blk, LANES), lambda b, i: (b, i, 0)),
                  pl.BlockSpec((HALO, 3 * GROUP_WIDTH), fixed),
                  pl.BlockSpec((1, LANES), fixed),
                  pl.BlockSpec((1, LANES), fixed),
                  pl.BlockSpec((GROUP_WIDTH, GROUP_WIDTH), fixed)],
        out_specs=[pl.BlockSpec((None, blk, 3 * GROUP_WIDTH), lambda b, i: (b, i, 0)),
                   pl.BlockSpec((None, blk, LANES), lambda b, i: (b, i, 0))],
        out_shape=[jax.ShapeDtypeStruct((bsz, seq, 3 * GROUP_WIDTH), BF16),
                   jax.ShapeDtypeStruct((bsz, seq, LANES), F32)],
        compiler_params=_cparams(("parallel", "parallel")),
        name="dn_prep",
    )(qkv3, qkv3, qkv3, misc3, convw, decay_a, dtb, headsum)
    return qkvn, gb


def _unit_triangular_inverses(mats, rows, cols, lowers):
    c = mats[0].shape[0]
    eye = (rows == cols).astype(F32)

    def level_mask(lower, s):
        hi, lo = (rows, cols) if lower else (cols, rows)
        return ((hi // s) == (lo // s) + 1) & ((hi // (2 * s)) == (lo // (2 * s)))

    ts = [eye - jnp.where(level_mask(lower, 1), a, 0.0) for a, lower in zip(mats, lowers)]
    s = 2
    while s < c:
        tc = [_dot(t, jnp.where(level_mask(lower, s), a, 0.0)) for t, a, lower in zip(ts, mats, lowers)]
        ts = [t - _dot(x, t) for t, x in zip(ts, tc)]
        s *= 2
    return ts


def _dn_main_kernel(qkv_f_ref, gb_f_ref, qkv_b_ref, gb_b_ref, o_f_ref, o_b_ref, state_ref):
    blk = qkv_f_ref.shape[0]
    nc = blk // DN_CHUNK
    c_ = DN_CHUNK

    @pl.when(pl.program_id(1) == 0)
    def _():
        state_ref[...] = jnp.zeros_like(state_ref)

    rows = lax.broadcasted_iota(jnp.int32, (c_, c_), 0)
    cols = lax.broadcasted_iota(jnp.int32, (c_, c_), 1)
    tri_lo = (rows >= cols).astype(F32)
    tri_up = (rows <= cols).astype(F32)
    n_rec = 2 * DN_HEADS
    lowers = [i < DN_HEADS for i in range(n_rec)]

    def group_step(gi, carry):
        r0s, qs, ks, kbs, vbs, gams, e_gcs, e_rests, e_tots = [], [], [], [], [], [], [], [], []
        for g in range(DN_GROUP):
            ci = gi * DN_GROUP + g
            for d in range(2):
                lower = d == 0
                qkv_ref, gb_ref = (qkv_f_ref, gb_f_ref) if lower else (qkv_b_ref, gb_b_ref)
                r0 = pl.multiple_of((ci if lower else nc - 1 - ci) * c_, c_)
                r0s.append(r0)
                x = qkv_ref[pl.ds(r0, c_), :].astype(F32)
                gbt = gb_ref[pl.ds(r0, c_), :]
                incl = (rows >= cols) if lower else (rows <= cols)
                gc = jnp.dot(tri_lo if lower else tri_up, gbt, precision=lax.Precision.HIGHEST,
                             preferred_element_type=F32)
                gct = jnp.concatenate([gc, jnp.zeros((LANES - c_, LANES), F32)], axis=0).T[:, :c_]
                g_last = gc[c_ - 1:c_, :] if lower else gc[0:1, :]
                e_gc = jnp.exp(gc)
                e_rest = jnp.exp(g_last - gc)
                e_tot = jnp.exp(g_last)
                for h in range(DN_HEADS):
                    lb = DN_HEADS * d + h
                    lg = 2 * DN_HEADS + lb
                    k = x[:, GROUP_WIDTH + h * DN_DK:GROUP_WIDTH + (h + 1) * DN_DK]
                    beta = gbt[:, lb:lb + 1]
                    diff = gc[:, lg:lg + 1] - gct[lg:lg + 1, :]
                    qs.append(x[:, h * DN_DK:(h + 1) * DN_DK])
                    ks.append(k)
                    kbs.append(k * beta)
                    vbs.append(x[:, 2 * GROUP_WIDTH + h * DN_DK:2 * GROUP_WIDTH + (h + 1) * DN_DK] * beta)
                    gams.append(jnp.where(incl, jnp.exp(jnp.where(incl, diff, 0.0)), 0.0))
                    e_gcs.append(e_gc[:, lg:lg + 1])
                    e_rests.append(e_rest[:, lg:lg + 1])
                    e_tots.append(e_tot[:, lg:lg + 1])
        every = range(DN_GROUP * n_rec)
        low = [lowers[i % n_rec] for i in every]
        gkk = [_dot_nt(kbs[i], ks[i]) for i in every]
        gqk = [_dot_nt(qs[i], ks[i]) for i in every]
        a = [jnp.where((rows > cols) if low[i] else (rows < cols), gkk[i] * gams[i], 0.0) for i in every]
        qk = [gqk[i] * gams[i] for i in every]
        t = _unit_triangular_inverses(a, rows, cols, low)
        uw = [_dot(t[i], jnp.concatenate([vbs[i], kbs[i] * e_gcs[i]], axis=1)) for i in every]
        states = [state_ref[r] for r in range(n_rec)]
        for g in range(DN_GROUP):
            rec = range(g * n_rec, (g + 1) * n_rec)
            qws = [_dot(jnp.concatenate([qs[i] * e_gcs[i], uw[i][:, DN_DK:]], axis=0), states[i % n_rec]) for i in rec]
            v_new = [uw[i][:, :DN_DK] - x[c_:, :] for i, x in zip(rec, qws)]
            outs = [x[:c_, :] + _dot(qk[i], vn) for i, x, vn in zip(rec, qws, v_new)]
            upd = [_dot_tn(ks[i] * e_rests[i], vn) for i, vn in zip(rec, v_new)]
            states = [states[i % n_rec] * e_tots[i] + u for i, u in zip(rec, upd)]
            o_f_ref[pl.ds(r0s[2 * g], c_), :] = jnp.concatenate(outs[:DN_HEADS], axis=1)
            o_b_ref[pl.ds(r0s[2 * g + 1], c_), :] = jnp.concatenate(outs[DN_HEADS:], axis=1)
        for r in range(n_rec):
            state_ref[r] = states[r]
        return carry

    lax.fori_loop(0, nc // DN_GROUP, group_step, 0)


def _dn_main(qkvn, gb, bsz, seq):
    blk = SEQ_TILE
    nb = seq // blk
    fwd = lambda b, i: (b, i, 0)
    bwd = lambda b, i: (b, nb - 1 - i, 0)
    return pl.pallas_call(
        _dn_main_kernel,
        grid=(bsz, nb),
        in_specs=[pl.BlockSpec((None, blk, 3 * GROUP_WIDTH), fwd),
                  pl.BlockSpec((None, blk, LANES), fwd),
                  pl.BlockSpec((None, blk, 3 * GROUP_WIDTH), bwd),
                  pl.BlockSpec((None, blk, LANES), bwd)],
        out_specs=[pl.BlockSpec((None, blk, GROUP_WIDTH), fwd),
                   pl.BlockSpec((None, blk, GROUP_WIDTH), bwd)],
        out_shape=[jax.ShapeDtypeStruct((bsz, seq, GROUP_WIDTH), F32)] * 2,
        scratch_shapes=[pltpu.VMEM((2 * DN_HEADS, DN_DK, DN_DK), F32)],
        compiler_params=_cparams(("parallel", "arbitrary")),
        name="dn_main",
    )(qkvn, gb, qkvn, gb)


def _sgu_kernel(x_ref, g_ref, b_ref, w_ref, bias_ref, o_ref):
    blk = x_ref.shape[0]
    x = x_ref[...].astype(F32)
    x = 0.5 * x * (1.0 + jnp.tanh(np.sqrt(2.0 / np.pi) * (x + 0.044715 * x * x * x)))
    u = x[:, :SGU_WIDTH]
    v = _layer_norm(x[:, SGU_WIDTH:], g_ref[...], b_ref[...]).astype(BF16)
    group = lax.broadcasted_iota(jnp.int32, (SGU_CHUNK, SGU_WIDTH), 1) // (SGU_WIDTH // SGU_GROUPS)
    for c in range(blk // SGU_CHUNK):
        r = slice(c * SGU_CHUNK, (c + 1) * SGU_CHUNK)
        vc = v[r, :]
        mixed = bias_ref[...]
        for gi in range(SGU_GROUPS):
            mixed = mixed + jnp.where(group == gi, _dot(w_ref[gi], vc), 0.0)
        o_ref[r, :] = (u[r, :] * mixed).astype(BF16)


def _sgu(x, ln_g, ln_b, w_s, b_s, bsz, seq):
    t = x.shape[0]
    blk = SEQ_TILE
    bias = jnp.repeat(b_s.T.astype(F32), SGU_WIDTH // SGU_GROUPS, axis=1)
    fixed2 = lambda i: (0, 0)
    return pl.pallas_call(
        _sgu_kernel,
        grid=(t // blk,),
        in_specs=[pl.BlockSpec((blk, 2 * SGU_WIDTH), lambda i: (i, 0)),
                  pl.BlockSpec((1, SGU_WIDTH), fixed2),
                  pl.BlockSpec((1, SGU_WIDTH), fixed2),
                  pl.BlockSpec((SGU_GROUPS, SGU_CHUNK, SGU_CHUNK), lambda i: (0, 0, 0)),
                  pl.BlockSpec((SGU_CHUNK, SGU_WIDTH), fixed2)],
        out_specs=pl.BlockSpec((blk, SGU_WIDTH), lambda i: (i, 0)),
        out_shape=jax.ShapeDtypeStruct((t, SGU_WIDTH), BF16),
        compiler_params=_cparams(("parallel",)),
        name="sgu",
    )(x, ln_g.reshape(1, -1).astype(F32), ln_b.reshape(1, -1).astype(F32), w_s.astype(BF16), bias)


def _pool_kernel(seq, prev_ref, main_ref, next_ref, w_ref, scale_ref, o_ref):
    blk = main_ref.shape[0]
    n = blk + 2 * HALO
    xh = _with_halo(prev_ref, main_ref, next_ref)
    shift = lambda a, k: pltpu.roll(a, k % n, 0)
    s2 = shift(xh, 1) + xh
    s4 = shift(s2, 1) + shift(s2, -1)
    s8 = shift(s4, 2) + shift(s4, -2)
    s16 = shift(s8, 4) + shift(s8, -4)
    sums = (s2, s4, s8, s16)
    x = xh[HALO:HALO + blk, :]
    group = lax.broadcasted_iota(jnp.int32, (blk, GROUP_WIDTH), 1) // POOL_GW
    pos = lax.broadcasted_iota(jnp.int32, (blk, GROUP_WIDTH), 0) + pl.program_id(1) * blk
    wsum = jnp.zeros((blk, GROUP_WIDTH), F32)
    cnt = jnp.ones((blk, GROUP_WIDTH), F32)
    for gi, win in enumerate(POOL_WINDOWS):
        hi = jnp.minimum(pos + win // 2, seq)
        lo = jnp.maximum(pos - win // 2, 0)
        wsum = jnp.where(group == gi, sums[gi][HALO:HALO + blk, :], wsum)
        cnt = jnp.where(group == gi, (hi - lo).astype(F32), cnt)
    pooled = wsum / cnt - x
    o_ref[...] = (_dot(pooled, w_ref[...]) * scale_ref[...]).astype(BF16)


def _pool(x, w_pool, scale, bsz, seq):
    blk = SEQ_TILE
    nb = seq // blk
    x3 = x.reshape(bsz, seq, GROUP_WIDTH)
    prev, main, nxt = _halo_specs(blk, GROUP_WIDTH, nb)
    wbd = jax.scipy.linalg.block_diag(*[w_pool[g].astype(F32) for g in range(len(POOL_WINDOWS))])
    fixed = lambda b, i: (0, 0)
    out = pl.pallas_call(
        functools.partial(_pool_kernel, seq),
        grid=(bsz, nb),
        in_specs=[prev, main, nxt,
                  pl.BlockSpec((GROUP_WIDTH, GROUP_WIDTH), fixed),
                  pl.BlockSpec((1, GROUP_WIDTH), fixed)],
        out_specs=pl.BlockSpec((None, blk, GROUP_WIDTH), lambda b, i: (b, i, 0)),
        out_shape=jax.ShapeDtypeStruct((bsz, seq, GROUP_WIDTH), BF16),
        compiler_params=_cparams(("parallel", "parallel")),
        name="pool",
    )(x3, x3, x3, wbd, scale.reshape(1, -1).astype(F32))
    return out.reshape(bsz * seq, GROUP_WIDTH)


def _rms(x, g):
    return x * lax.rsqrt(jnp.mean(x * x, axis=-1, keepdims=True) + RMS_EPS) * g


def _mla_prep_kernel(x_ref, qg_ref, kvg_ref, wq_ref, wk_ref, wvt_ref, cos_ref, sin_ref, q_ref, k_ref, vt_ref):
    hw = MLA_HEADS * LANES
    cq = _rms(x_ref[:, 0:MLA_Q_RANK].astype(F32), qg_ref[...]).astype(BF16)
    ckv = _rms(x_ref[:, MLA_Q_RANK:MLA_Q_RANK + MLA_KV_RANK].astype(F32), kvg_ref[...]).astype(BF16)
    c0 = MLA_Q_RANK + MLA_KV_RANK
    cos = cos_ref[...]
    sin = sin_ref[...]
    k_rope = (x_ref[:, c0:c0 + LANES].astype(F32) * cos
              + x_ref[:, c0 + LANES:c0 + 2 * LANES].astype(F32) * sin)
    qa = _dot(cq, wq_ref[:, 0:hw])
    qb = _dot(cq, wq_ref[:, hw:2 * hw])
    kn = _dot(ckv, wk_ref[...])
    vt = _dot_nt(wvt_ref[...], ckv).astype(BF16)
    ones = jnp.ones((MLA_VROWS - MLA_V, vt.shape[1]), BF16)
    vt = jnp.concatenate([piece for h in range(MLA_HEADS)
                          for piece in (vt[h * MLA_V:(h + 1) * MLA_V, :], ones)], axis=0)
    for c in range(vt_ref.shape[0]):
        vt_ref[c] = vt[:, c * ATT_TK:(c + 1) * ATT_TK]
    for h in range(MLA_HEADS):
        c = slice(h * LANES, (h + 1) * LANES)
        q_ref[:, c] = (qa[:, c] * cos + qb[:, c] * sin).astype(BF16)
        k_ref[:, c] = (kn[:, c] + k_rope).astype(BF16)


def _rope_tables(seq):
    inv = ROPE_THETA ** (-jnp.arange(0, MLA_ROPE, 2, dtype=F32) / MLA_ROPE)
    ang = jnp.arange(seq, dtype=F32)[:, None] * inv[None, :]
    pad = LANES - MLA_NOPE - MLA_ROPE
    cos = jnp.concatenate([jnp.ones((seq, MLA_NOPE), F32), jnp.cos(ang), jnp.cos(ang), jnp.zeros((seq, pad), F32)], 1)
    sin = jnp.concatenate([jnp.zeros((seq, MLA_NOPE), F32), jnp.sin(ang), jnp.sin(ang), jnp.zeros((seq, pad), F32)], 1)
    return cos, sin


def _pack_mla_weights(w_uq, w_uk, w_uv):
    half = MLA_ROPE // 2
    scale = (MLA_NOPE + MLA_ROPE) ** -0.5 * np.log2(np.e)
    pad = LANES - MLA_NOPE - MLA_ROPE
    qa, qb, kn = [], [], []
    for h in range(MLA_HEADS):
        wq = w_uq[:, h, :].astype(F32) * scale
        rope = wq[:, MLA_NOPE:]
        rot = jnp.concatenate([-rope[:, half:], rope[:, :half]], axis=1)
        zq = lambda n: jnp.zeros((MLA_Q_RANK, n), F32)
        qa.append(jnp.concatenate([wq, zq(pad)], axis=1))
        qb.append(jnp.concatenate([zq(MLA_NOPE), rot, zq(pad)], axis=1))
        kn.append(jnp.concatenate([w_uk[:, h, :].astype(F32), jnp.zeros((MLA_KV_RANK, LANES - MLA_NOPE), F32)], axis=1))
    wq_packed = jnp.concatenate(qa + qb, axis=1).astype(BF16)
    wk_packed = jnp.concatenate(kn, axis=1).astype(BF16)
    wv_t = w_uv.reshape(MLA_KV_RANK, MLA_HEADS * MLA_V).T.astype(BF16)
    return wq_packed, wk_packed, wv_t


def _mla_prep(x, q_g, kv_g, wq_packed, wk_packed, wv_t, cos, sin, bsz, seq):
    t = x.shape[0]
    tm = ROW_TILE
    per_seq = seq // tm
    per_tile = tm // ATT_TK
    hw = MLA_HEADS * LANES
    row = lambda i: (i, 0)
    fixed = lambda i: (0, 0)
    table = lambda i: (i % per_seq, 0)
    return pl.pallas_call(
        _mla_prep_kernel,
        grid=(t // tm,),
        in_specs=[pl.BlockSpec((tm, 768), row),
                  pl.BlockSpec((1, MLA_Q_RANK), fixed),
                  pl.BlockSpec((1, MLA_KV_RANK), fixed),
                  pl.BlockSpec((MLA_Q_RANK, 2 * hw), fixed),
                  pl.BlockSpec((MLA_KV_RANK, hw), fixed),
                  pl.BlockSpec((GROUP_WIDTH, MLA_KV_RANK), fixed),
                  pl.BlockSpec((tm, LANES), table),
                  pl.BlockSpec((tm, LANES), table)],
        out_specs=[pl.BlockSpec((tm, hw), row),
                   pl.BlockSpec((tm, hw), row),
                   pl.BlockSpec((None, per_tile, MLA_HEADS * MLA_VROWS, ATT_TK),
                                lambda i: (i // per_seq, i % per_seq, 0, 0))],
        out_shape=[jax.ShapeDtypeStruct((t, hw), BF16),
                   jax.ShapeDtypeStruct((t, hw), BF16),
                   jax.ShapeDtypeStruct((bsz, seq // ATT_TK, MLA_HEADS * MLA_VROWS, ATT_TK), BF16)],
        compiler_params=_cparams(("parallel",)),
        name="mla_prep",
    )(x, q_g.reshape(1, -1).astype(F32), kv_g.reshape(1, -1).astype(F32), wq_packed, wk_packed, wv_t, cos, sin)


def _attn_kernel(q_ref, k_ref, vt_ref, o_ref):
    n_steps = vt_ref.shape[0]
    tq = q_ref.shape[0]
    tk = ATT_TK
    unroll = min(n_steps, ATT_UNROLL)
    qs = [q_ref[:, h * LANES:(h + 1) * LANES] for h in range(2)]

    def scores(h, j):
        r0 = pl.multiple_of(j * tk, tk)
        return _dot_nt(k_ref[pl.ds(r0, tk), h * LANES:(h + 1) * LANES], qs[h])

    def update(h, j, s, m, acc):
        m_new = jnp.maximum(m, jnp.max(s, axis=0, keepdims=True))
        p = jnp.exp2(s - m_new).astype(BF16)
        pv = _dot(vt_ref[j, h * MLA_VROWS:(h + 1) * MLA_VROWS, :], p)
        return m_new, acc * jnp.exp2(m - m_new) + pv

    def group(g, carry):
        base = g * unroll
        state = [list(carry[0:2]), list(carry[2:4])]
        pending = [[scores(h, base + u) for h in range(2)] for u in range(min(ATT_LOOKAHEAD, unroll))]
        for u in range(unroll):
            s_cur = pending.pop(0)
            if u + ATT_LOOKAHEAD < unroll:
                pending.append([scores(h, base + u + ATT_LOOKAHEAD) for h in range(2)])
            for h in range(2):
                state[h] = list(update(h, base + u, s_cur[h], *state[h]))
        return tuple(state[0] + state[1])

    one = (jnp.full((1, tq), -jnp.inf, F32), jnp.zeros((MLA_VROWS, tq), F32))
    _, acc0, _, acc1 = lax.fori_loop(0, n_steps // unroll, group, one + one)
    out_t = jnp.concatenate([acc[:MLA_V, :] / acc[MLA_V:MLA_V + 1, :] for acc in (acc0, acc1)], axis=0)
    o_ref[...] = out_t.T.astype(BF16)


def _attention(q, k, vt, bsz, seq):
    hw = MLA_HEADS * LANES
    q3, k3 = (a.reshape(bsz, seq, hw) for a in (q, k))
    tq = ATT_TQ
    out = pl.pallas_call(
        _attn_kernel,
        grid=(bsz, 2, seq // tq),
        in_specs=[pl.BlockSpec((None, tq, 2 * LANES), lambda b, p, i: (b, i, p)),
                  pl.BlockSpec((None, seq, 2 * LANES), lambda b, p, i: (b, 0, p)),
                  pl.BlockSpec((None, seq // ATT_TK, 2 * MLA_VROWS, ATT_TK), lambda b, p, i: (b, 0, p, 0))],
        out_specs=pl.BlockSpec((None, tq, LANES), lambda b, p, i: (b, i, p)),
        out_shape=jax.ShapeDtypeStruct((bsz, seq, GROUP_WIDTH), BF16),
        compiler_params=_cparams(("parallel", "parallel", "arbitrary"), 48),
        name="mla_attention",
    )(q3, k3, vt)
    return out.reshape(bsz * seq, GROUP_WIDTH)


def _out_proj_kernel(pre_ln, x_ref, lng_ref, lnb_ref, of_ref, ob_ref, z_ref, ng_ref, headmean_ref,
                     yb_ref, yc_ref, yd_ref, w_ref, g_ref, b_ref, o_ref):
    x = x_ref[...]
    if pre_ln:
        x = _layer_norm(x, lng_ref[...], lnb_ref[...])
    o = of_ref[...] + ob_ref[...]
    z = z_ref[...].astype(F32)
    ya = o * lax.rsqrt(_dot(o * o, headmean_ref[...]) + RMS_EPS) * ng_ref[...] * _silu(z)
    gw = GROUP_WIDTH
    mix = (_dot(ya.astype(BF16), w_ref[0:gw, :]) + _dot(yb_ref[...], w_ref[gw:2 * gw, :])
           + _dot(yc_ref[...], w_ref[2 * gw:3 * gw, :]) + _dot(yd_ref[...], w_ref[3 * gw:, :]))
    o_ref[...] = _layer_norm(DEEPNORM_ALPHA * x + mix, g_ref[...], b_ref[...])


def _out_proj(x, ln_in_g, ln_in_b, o_f, o_b, z, norm_g, y_b, y_c, y_d, w_out, ln_g, ln_b, pre_ln):
    t = x.shape[0]
    tm = ROW_TILE
    head = np.arange(GROUP_WIDTH) // DN_DK
    headmean = jnp.asarray((head[:, None] == head[None, :]).astype(np.float32) / DN_DK)
    ng = jnp.tile(norm_g.astype(F32), DN_HEADS).reshape(1, GROUP_WIDTH)
    row = lambda i: (i, 0)
    fixed = lambda i: (0, 0)
    quarter = pl.BlockSpec((tm, GROUP_WIDTH), row)
    vec = pl.BlockSpec((1, D_MODEL), fixed)
    return pl.pallas_call(
        functools.partial(_out_proj_kernel, pre_ln),
        grid=(t // tm,),
        in_specs=[pl.BlockSpec((tm, D_MODEL), row), vec, vec,
                  quarter, quarter, quarter,
                  pl.BlockSpec((1, GROUP_WIDTH), fixed),
                  pl.BlockSpec((GROUP_WIDTH, GROUP_WIDTH), fixed),
                  quarter, quarter, quarter,
                  pl.BlockSpec((D_MODEL, D_MODEL), fixed), vec, vec],
        out_specs=pl.BlockSpec((tm, D_MODEL), row),
        out_shape=jax.ShapeDtypeStruct((t, D_MODEL), F32),
        compiler_params=_cparams(("parallel",)),
        name="out_proj",
    )(x, ln_in_g, ln_in_b, o_f, o_b, z, ng, headmean, y_b, y_c, y_d, w_out.astype(BF16),
      ln_g.reshape(1, -1).astype(F32), ln_b.reshape(1, -1).astype(F32))


def _router_kernel(x_ref, whi_ref, wlo_ref, bias_ref, cls_ref, rank_ref, cnt_ref, carry_ref):
    tm = x_ref.shape[0]

    @pl.when(pl.program_id(0) == 0)
    def _():
        carry_ref[...] = jnp.zeros_like(carry_ref)

    x = x_ref[...]
    x_hi = x.astype(BF16)
    x_lo = (x - x_hi.astype(F32)).astype(BF16)
    logits = _dot(x_hi, whi_ref[...]) + (_dot(x_lo, whi_ref[...]) + _dot(x_hi, wlo_ref[...]))
    lane = lax.broadcasted_iota(jnp.int32, (tm, LANES), 1)
    valid = lane < N_EXPERTS
    biased = jnp.where(valid, _sigmoid(logits) + bias_ref[...], -jnp.inf)
    member = lane % EXPERTS_PER_GROUP
    neg = jnp.full((tm, LANES), -jnp.inf, F32)
    best_pair = neg
    beaten = jnp.zeros((tm, LANES), jnp.int32)
    others = []
    for k in range(1, EXPERTS_PER_GROUP):
        wrapped = member + k >= EXPERTS_PER_GROUP
        other = jnp.where(wrapped, pltpu.roll(biased, EXPERTS_PER_GROUP - k, 1),
                          pltpu.roll(biased, LANES - k, 1))
        others.append(other)
        best_pair = jnp.maximum(best_pair, biased + other)
        beaten = beaten + ((other > biased) | ((other == biased) & wrapped)).astype(jnp.int32)
    best_pair = jnp.maximum(best_pair, jnp.maximum(others[0] + others[1],
                                                   jnp.maximum(others[0] + others[2], others[1] + others[2])))
    group_score = jnp.where(valid, best_pair, -jnp.inf)
    top = jnp.max(group_score, axis=-1, keepdims=True)
    group = lane // EXPERTS_PER_GROUP
    chosen = jnp.min(jnp.where(group_score == top, group, N_EXPERT_GROUPS), axis=-1, keepdims=True)
    picked = valid & (group == chosen) & (beaten < 2)
    e_lo = jnp.min(jnp.where(picked, lane, LANES), axis=-1, keepdims=True)
    e_hi = jnp.max(jnp.where(picked, lane, -1), axis=-1, keepdims=True)
    i = e_lo % EXPERTS_PER_GROUP
    j = e_hi % EXPERTS_PER_GROUP
    cls = chosen * len(PAIRS) + (i * (7 - i)) // 2 + (j - i - 1)
    cls_t = jnp.broadcast_to(cls.astype(F32), (tm, LANES)).T
    cls_row = cls_t[0:1, :]
    onehot = (lax.broadcasted_iota(jnp.int32, (CLASS_ROWS, tm), 0).astype(F32) == cls_row)
    upto = (lax.broadcasted_iota(jnp.int32, (tm, tm), 0) <= lax.broadcasted_iota(jnp.int32, (tm, tm), 1))
    prefix = _dot(onehot.astype(BF16), upto.astype(BF16))
    carry = carry_ref[:, 0:1]
    rank = jnp.sum(jnp.where(onehot, prefix + carry, 0.0), axis=0, keepdims=True) - 1.0
    cls_ref[...] = cls_t[0:HALO, :]
    rank_ref[...] = jnp.broadcast_to(rank, (HALO, tm))
    total = carry + prefix[:, tm - 1:tm]
    carry_ref[...] = jnp.broadcast_to(total, carry_ref.shape)
    cnt_ref[...] = jnp.broadcast_to(total, cnt_ref.shape)


def _router(x, router_w, router_bias):
    t = x.shape[0]
    tm = ROW_TILE
    w = jnp.zeros((D_MODEL, LANES), F32).at[:, :N_EXPERTS].set(router_w.astype(F32))
    w_hi = w.astype(BF16)
    w_lo = (w - w_hi.astype(F32)).astype(BF16)
    bias = jnp.zeros((1, LANES), F32).at[0, :N_EXPERTS].set(router_bias.astype(F32))
    fixed = lambda i: (0, 0)
    cls, rank, cnt = pl.pallas_call(
        _router_kernel,
        grid=(t // tm,),
        in_specs=[pl.BlockSpec((tm, D_MODEL), lambda i: (i, 0)),
                  pl.BlockSpec((D_MODEL, LANES), fixed),
                  pl.BlockSpec((D_MODEL, LANES), fixed),
                  pl.BlockSpec((1, LANES), fixed)],
        out_specs=[pl.BlockSpec((HALO, tm), lambda i: (0, i)),
                   pl.BlockSpec((HALO, tm), lambda i: (0, i)),
                   pl.BlockSpec((CLASS_ROWS, LANES), fixed)],
        out_shape=[jax.ShapeDtypeStruct((HALO, t), F32),
                   jax.ShapeDtypeStruct((HALO, t), F32),
                   jax.ShapeDtypeStruct((CLASS_ROWS, LANES), F32)],
        scratch_shapes=[pltpu.VMEM((CLASS_ROWS, LANES), F32)],
        compiler_params=_cparams(("arbitrary",)),
        name="router",
    )(x, w_hi, w_lo, bias)
    return cls[0].astype(jnp.int32), rank[0].astype(jnp.int32), cnt[:N_CLASSES, 0].astype(jnp.int32)


def _routing_tables(cls, rank, counts, n_tiles):
    tm = MOE_TILE
    tiles = (counts + tm - 1) // tm
    first_tile = jnp.cumsum(tiles) - tiles
    dest = first_tile[cls] * tm + rank
    tile_ids = jnp.arange(n_tiles, dtype=jnp.int32)
    tile_cls = jnp.sum((jnp.cumsum(tiles)[None, :] <= tile_ids[:, None]).astype(jnp.int32), axis=1)
    tile_cls = jnp.minimum(tile_cls, N_CLASSES - 1)
    used = (tile_ids < jnp.sum(tiles)).astype(jnp.int32)
    pair = np.asarray(PAIRS, np.int32)
    base = (tile_cls // len(PAIRS)) * EXPERTS_PER_GROUP
    e1 = base + jnp.asarray(pair[:, 0])[tile_cls % len(PAIRS)]
    e2 = base + jnp.asarray(pair[:, 1])[tile_cls % len(PAIRS)]
    return dest.astype(jnp.int32), e1.astype(jnp.int32), e2.astype(jnp.int32), used


def _dispatch_kernel(dest_ref, x_ref, zeros_ref, xs_ref, sem):
    del zeros_ref
    tr = x_ref.shape[0]

    def row_copy(r):
        return pltpu.make_async_copy(x_ref.at[pl.ds(r, 1), :], xs_ref.at[pl.ds(dest_ref[0, 0, r], 1), :], sem)

    for r in range(tr):
        row_copy(r).start()
    for r in range(tr):
        row_copy(r).wait()


def _dispatch(x, dest, n_slots):
    t = x.shape[0]
    tr = DMA_TILE
    dest3 = dest.reshape(t // tr, 1, tr)
    zeros = jnp.zeros((n_slots, D_MODEL), F32)
    return pl.pallas_call(
        _dispatch_kernel,
        grid=(t // tr,),
        in_specs=[pl.BlockSpec((1, 1, tr), lambda i: (i, 0, 0), memory_space=pltpu.SMEM),
                  pl.BlockSpec((tr, D_MODEL), lambda i: (i, 0)),
                  pl.BlockSpec(memory_space=pl.ANY)],
        out_specs=pl.BlockSpec(memory_space=pl.ANY),
        out_shape=jax.ShapeDtypeStruct((n_slots, D_MODEL), F32),
        scratch_shapes=[pltpu.SemaphoreType.DMA],
        input_output_aliases={2: 0},
        compiler_params=_cparams(("arbitrary",)),
        name="moe_dispatch",
    )(dest3, x, zeros)


def _moe_kernel(e1_ref, e2_ref, used_ref, x_ref, rw_ref, wg1_ref, wu1_ref, wd1_ref, wg2_ref, wu2_ref, wd2_ref, y_ref):
    i = pl.program_id(0)

    @pl.when(used_ref[i] == 1)
    def _():
        xb = x_ref[...].astype(BF16)
        scores = _sigmoid(_dot(xb, rw_ref[...]))
        lane = lax.broadcasted_iota(jnp.int32, scores.shape, 1)
        s1 = jnp.sum(jnp.where(lane == e1_ref[i], scores, 0.0), axis=-1, keepdims=True)
        s2 = jnp.sum(jnp.where(lane == e2_ref[i], scores, 0.0), axis=-1, keepdims=True)
        inv = 1.0 / (s1 + s2)
        h1 = _silu(_dot(xb, wg1_ref[...])) * _dot(xb, wu1_ref[...]) * (s1 * inv)
        h2 = _silu(_dot(xb, wg2_ref[...])) * _dot(xb, wu2_ref[...]) * (s2 * inv)
        y_ref[...] = _dot(h1.astype(BF16), wd1_ref[...]) + _dot(h2.astype(BF16), wd2_ref[...])

    @pl.when(used_ref[i] == 0)
    def _():
        y_ref[...] = jnp.zeros_like(y_ref)


def _moe(xs, e1, e2, used, router_w, w_gate, w_up, w_down):
    n_slots = xs.shape[0]
    tm = MOE_TILE
    rw = jnp.zeros((D_MODEL, LANES), BF16).at[:, :N_EXPERTS].set(router_w.astype(BF16))
    up = lambda sel: pl.BlockSpec((None, D_MODEL, EXPERT_FF), lambda i, e1, e2, u: ((e1, e2)[sel][i], 0, 0))
    down = lambda sel: pl.BlockSpec((None, EXPERT_FF, D_MODEL), lambda i, e1, e2, u: ((e1, e2)[sel][i], 0, 0))
    grid_spec = pltpu.PrefetchScalarGridSpec(
        num_scalar_prefetch=3,
        grid=(n_slots // tm,),
        in_specs=[pl.BlockSpec((tm, D_MODEL), lambda i, e1, e2, u: (i, 0)),
                  pl.BlockSpec((D_MODEL, LANES), lambda i, e1, e2, u: (0, 0)),
                  up(0), up(0), down(0), up(1), up(1), down(1)],
        out_specs=pl.BlockSpec((tm, D_MODEL), lambda i, e1, e2, u: (i, 0)),
    )
    return pl.pallas_call(
        _moe_kernel,
        grid_spec=grid_spec,
        out_shape=jax.ShapeDtypeStruct((n_slots, D_MODEL), F32),
        compiler_params=_cparams(("arbitrary",), 48),
        name="moe_experts",
    )(e1, e2, used, xs, rw, w_gate, w_up, w_down, w_gate, w_up, w_down)


def _combine_kernel(dest_ref, x_ref, ys_ref, g_ref, b_ref, o_ref, buf, sem):
    tr = x_ref.shape[0]

    def row_copy(r):
        return pltpu.make_async_copy(ys_ref.at[pl.ds(dest_ref[0, 0, r], 1), :], buf.at[pl.ds(r, 1), :], sem)

    for r in range(tr):
        row_copy(r).start()
    for r in range(tr):
        row_copy(r).wait()
    o_ref[...] = _layer_norm(DEEPNORM_ALPHA * x_ref[...] + buf[...], g_ref[...], b_ref[...])


def _combine(x, ys, dest, ln_g, ln_b):
    t = x.shape[0]
    tr = DMA_TILE
    dest3 = dest.reshape(t // tr, 1, tr)
    fixed = lambda i: (0, 0)
    return pl.pallas_call(
        _combine_kernel,
        grid=(t // tr,),
        in_specs=[pl.BlockSpec((1, 1, tr), lambda i: (i, 0, 0), memory_space=pltpu.SMEM),
                  pl.BlockSpec((tr, D_MODEL), lambda i: (i, 0)),
                  pl.BlockSpec(memory_space=pl.ANY),
                  pl.BlockSpec((1, D_MODEL), fixed),
                  pl.BlockSpec((1, D_MODEL), fixed)],
        out_specs=pl.BlockSpec((tr, D_MODEL), lambda i: (i, 0)),
        out_shape=jax.ShapeDtypeStruct((t, D_MODEL), F32),
        scratch_shapes=[pltpu.VMEM((tr, D_MODEL), F32), pltpu.SemaphoreType.DMA],
        compiler_params=_cparams(("arbitrary",)),
        name="moe_combine",
    )(dest3, x, ys, ln_g.reshape(1, -1).astype(F32), ln_b.reshape(1, -1).astype(F32))


def _trunk(x, p):
    bsz, seq, d = x.shape
    t = bsz * seq
    x = x.reshape(t, d)
    cos, sin = _rope_tables(seq)
    n_tiles = t // MOE_TILE + N_CLASSES
    for l in range(DEPTH):
        first = l == 0
        qkv, z, sgu_in, pool_in, mla_in, misc = _in_proj(x, p["ln_in_g"], p["ln_in_b"], p["w_in"][l], first)
        qkvn, gb = _dn_prep(qkv, misc, p["dn_conv_w"][l], p["dn_a_log"][l], p["dn_dt_bias"][l], bsz, seq)
        o_f, o_b = _dn_main(qkvn, gb, bsz, seq)
        y_b = _sgu(sgu_in, p["sgu_ln_g"][l], p["sgu_ln_b"][l], p["sgu_w"][l], p["sgu_b"][l], bsz, seq)
        y_c = _pool(pool_in, p["pool_w"][l], p["pool_scale"][l], bsz, seq)
        q, k, vt = _mla_prep(mla_in, p["mla_q_norm_g"][l], p["mla_kv_norm_g"][l], p["wq"][l], p["wk"][l],
                             p["wvt"][l], cos, sin, bsz, seq)
        y_d = _attention(q, k, vt, bsz, seq)
        x = _out_proj(x, p["ln_in_g"], p["ln_in_b"], o_f.reshape(t, -1), o_b.reshape(t, -1), z,
                      p["dn_norm_g"][l], y_b, y_c, y_d, p["w_out"][l], p["ln1_g"][l], p["ln1_b"][l], first)
        cls, rank, counts = _router(x, p["router_w"], p["router_bias"])
        dest, e1, e2, used = _routing_tables(cls, rank, counts, n_tiles)
        xs = _dispatch(x, dest, n_tiles * MOE_TILE)
        ys = _moe(xs, e1, e2, used, p["router_w"], p["moe_w_gate"][l], p["moe_w_up"][l], p["moe_w_down"][l])
        x = _combine(x, ys, dest, p["ln2_g"][l], p["ln2_b"][l])
    return x.reshape(bsz, seq, d)


def kernel(x_prompt, x_sample, ln_in_g, ln_in_b, w_in, dn_conv_w, dn_a_log, dn_dt_bias, dn_norm_g, sgu_ln_g, sgu_ln_b, sgu_w, sgu_b, pool_w, pool_scale, mla_q_norm_g, mla_kv_norm_g, mla_w_uq, mla_w_uk, mla_w_uv, w_out, ln1_g, ln1_b, router_w, router_bias, moe_w_gate, moe_w_up, moe_w_down, ln2_g, ln2_b):
    packed_mla = [_pack_mla_weights(mla_w_uq[l], mla_w_uk[l], mla_w_uv[l]) for l in range(DEPTH)]
    p = dict(
        ln_in_g=ln_in_g.reshape(1, -1).astype(F32), ln_in_b=ln_in_b.reshape(1, -1).astype(F32),
        w_in=[_pack_in_weights(w_in[l]) for l in range(DEPTH)],
        dn_conv_w=dn_conv_w, dn_a_log=dn_a_log, dn_dt_bias=dn_dt_bias, dn_norm_g=dn_norm_g,
        sgu_ln_g=sgu_ln_g, sgu_ln_b=sgu_ln_b, sgu_w=sgu_w, sgu_b=sgu_b,
        pool_w=pool_w, pool_scale=pool_scale,
        mla_q_norm_g=mla_q_norm_g, mla_kv_norm_g=mla_kv_norm_g,
        wq=[w[0] for w in packed_mla], wk=[w[1] for w in packed_mla], wvt=[w[2] for w in packed_mla],
        w_out=w_out, ln1_g=ln1_g, ln1_b=ln1_b, router_w=router_w, router_bias=router_bias,
        moe_w_gate=moe_w_gate.astype(BF16), moe_w_up=moe_w_up.astype(BF16), moe_w_down=moe_w_down.astype(BF16),
        ln2_g=ln2_g, ln2_b=ln2_b,
    )
    return _trunk(x_prompt, p), _trunk(x_sample, p)
```

```python
import functools

import numpy as np
import jax
import jax.numpy as jnp
from jax import lax
from jax.experimental import pallas as pl
from jax.experimental.pallas import tpu as pltpu

F32 = jnp.float32
BF16 = jnp.bfloat16

D_MODEL = 1024
DEPTH = 2
GROUP_WIDTH = D_MODEL // 4
DN_HEADS = 4
DN_DK = GROUP_WIDTH // DN_HEADS
DN_CONV = 5
DN_CHUNK = 64
DN_GROUP = 4
SGU_GROUPS = 4
SGU_CHUNK = 128
SGU_WIDTH = GROUP_WIDTH
POOL_WINDOWS = (2, 4, 8, 16)
POOL_GW = GROUP_WIDTH // len(POOL_WINDOWS)
MLA_HEADS = 4
MLA_NOPE = 64
MLA_ROPE = 32
MLA_V = GROUP_WIDTH // MLA_HEADS
MLA_VROWS = MLA_V + 16
MLA_Q_RANK = 384
MLA_KV_RANK = 128
ROPE_THETA = 10000.0
N_EXPERTS = 16
N_EXPERT_GROUPS = 4
EXPERTS_PER_GROUP = N_EXPERTS // N_EXPERT_GROUPS
EXPERT_FF = 512
DEEPNORM_ALPHA = (2 * DEPTH) ** 0.25
LN_EPS = 1e-5
RMS_EPS = 1e-6

LANES = 128
HALO = 8

PAIRS = ((0, 1), (0, 2), (0, 3), (1, 2), (1, 3), (2, 3))
N_CLASSES = N_EXPERT_GROUPS * len(PAIRS)
CLASS_ROWS = 32

C_QKV = 0
C_Z = 768
C_SGU = 1024
C_POOL = 1536
C_MLA = 1792
C_MISC = 2560
IN_PACKED = 2688

ROW_TILE = 512
SEQ_TILE = 512
MOE_TILE = 256
DMA_TILE = 512
ATT_TQ = 256
ATT_TK = 256
ATT_UNROLL = 8
ATT_LOOKAHEAD = 3


def _cparams(sem, vmem_mb=None):
    kw = dict(dimension_semantics=sem)
    if vmem_mb is not None:
        kw["vmem_limit_bytes"] = vmem_mb * 1024 * 1024
    return pltpu.CompilerParams(**kw)


def _layer_norm(x, g, b):
    mu = jnp.mean(x, axis=-1, keepdims=True)
    xc = x - mu
    var = jnp.mean(xc * xc, axis=-1, keepdims=True)
    return xc * lax.rsqrt(var + LN_EPS) * g + b


def _sigmoid(x):
    return 1.0 / (1.0 + jnp.exp(-x))


def _silu(x):
    return x * _sigmoid(x)


def _dot(a, b):
    return jnp.dot(a, b, preferred_element_type=F32)


def _dot_nt(a, b):
    return lax.dot_general(a, b, (((1,), (1,)), ((), ())), preferred_element_type=F32)


def _dot_tn(a, b):
    return lax.dot_general(a, b, (((0,), (0,)), ((), ())), preferred_element_type=F32)


def _in_proj_kernel(pre_ln, x_ref, g_ref, b_ref, w_ref,
                    qkv_ref, z_ref, sgu_ref, pool_ref, mla_ref, misc_ref):
    x = x_ref[...]
    if pre_ln:
        x = _layer_norm(x, g_ref[...], b_ref[...])
    xb = x.astype(BF16)
    qkv_ref[...] = _dot(xb, w_ref[:, C_QKV:C_Z]).astype(BF16)
    z_ref[...] = _dot(xb, w_ref[:, C_Z:C_SGU]).astype(BF16)
    sgu_ref[...] = _dot(xb, w_ref[:, C_SGU:C_POOL]).astype(BF16)
    pool_ref[...] = _dot(xb, w_ref[:, C_POOL:C_MLA]).astype(BF16)
    mla_ref[...] = _dot(xb, w_ref[:, C_MLA:C_MISC]).astype(BF16)
    misc_ref[...] = _dot(xb, w_ref[:, C_MISC:IN_PACKED])


def _in_proj(x, ln_g, ln_b, w_packed, pre_ln):
    t = x.shape[0]
    tm = ROW_TILE
    widths = (768, 256, 512, 256, 768, 128)
    dtypes = (BF16, BF16, BF16, BF16, BF16, F32)
    row = lambda i: (i, 0)
    fixed = lambda i: (0, 0)
    return pl.pallas_call(
        functools.partial(_in_proj_kernel, pre_ln),
        grid=(t // tm,),
        in_specs=[pl.BlockSpec((tm, D_MODEL), row),
                  pl.BlockSpec((1, D_MODEL), fixed),
                  pl.BlockSpec((1, D_MODEL), fixed),
                  pl.BlockSpec((D_MODEL, IN_PACKED), fixed)],
        out_specs=[pl.BlockSpec((tm, w), row) for w in widths],
        out_shape=[jax.ShapeDtypeStruct((t, w), d) for w, d in zip(widths, dtypes)],
        compiler_params=_cparams(("parallel",), 48),
        name="in_proj",
    )(x, ln_g, ln_b, w_packed)


def _pack_in_weights(w):
    d = w.shape[0]
    zeros = lambda n: jnp.zeros((d, n), w.dtype)
    qkv, z = w[:, 0:768], w[:, 768:1024]
    bb, aa = w[:, 1024:1032], w[:, 1032:1040]
    sgu, pool = w[:, 1040:1552], w[:, 1552:1808]
    cq, ckv, kr = w[:, 1808:2192], w[:, 2192:2320], w[:, 2320:2352]
    half = MLA_ROPE // 2
    kr_rot = jnp.concatenate([-kr[:, half:], kr[:, :half]], axis=1)
    rope_tile = jnp.concatenate([zeros(MLA_NOPE), kr, zeros(LANES - MLA_NOPE - MLA_ROPE)], axis=1)
    rot_tile = jnp.concatenate([zeros(MLA_NOPE), kr_rot, zeros(LANES - MLA_NOPE - MLA_ROPE)], axis=1)
    misc = jnp.concatenate([bb, aa, zeros(LANES - 16)], axis=1)
    return jnp.concatenate([qkv, z, sgu, pool, cq, ckv, rope_tile, rot_tile, misc], axis=1).astype(BF16)


def _halo_specs(blk, width, n_seq_blocks):
    per = blk // HALO
    last = n_seq_blocks * per - 1
    main = pl.BlockSpec((None, blk, width), lambda b, i: (b, i, 0))
    prev = pl.BlockSpec((None, HALO, width), lambda b, i: (b, jnp.maximum(i * per - 1, 0), 0))
    nxt = pl.BlockSpec((None, HALO, width), lambda b, i: (b, jnp.minimum((i + 1) * per, last), 0))
    return prev, main, nxt


def _with_halo(prev_ref, main_ref, next_ref):
    i = pl.program_id(1)
    n = pl.num_programs(1)
    prev = jnp.where(i > 0, prev_ref[...].astype(F32), 0.0)
    nxt = jnp.where(i < n - 1, next_ref[...].astype(F32), 0.0)
    return jnp.concatenate([prev, main_ref[...].astype(F32), nxt], axis=0)


def _dn_prep_kernel(prev_ref, main_ref, next_ref, misc_ref, convw_ref, decay_a_ref, dtb_ref, headsum_ref,
                    qkvn_ref, gb_ref):
    blk = main_ref.shape[0]
    xh = _with_halo(prev_ref, main_ref, next_ref)
    acc = jnp.zeros((blk, 3 * GROUP_WIDTH), F32)
    for j in range(DN_CONV):
        off = HALO - DN_CONV // 2 + j
        acc = acc + xh[off:off + blk, :] * convw_ref[j:j + 1, :]
    x = _silu(acc)
    q = x[:, 0:GROUP_WIDTH]
    k = x[:, GROUP_WIDTH:2 * GROUP_WIDTH]
    v = x[:, 2 * GROUP_WIDTH:]
    hs = headsum_ref[...]
    q = q * lax.rsqrt(_dot(q * q, hs) + RMS_EPS) * (DN_DK ** -0.5)
    k = k * lax.rsqrt(_dot(k * k, hs) + RMS_EPS)
    qkvn_ref[:, 0:GROUP_WIDTH] = q.astype(BF16)
    qkvn_ref[:, GROUP_WIDTH:2 * GROUP_WIDTH] = k.astype(BF16)
    qkvn_ref[:, 2 * GROUP_WIDTH:] = v.astype(BF16)
    m = misc_ref[...]
    lane = lax.broadcasted_iota(jnp.int32, m.shape, 1)
    xs = m + dtb_ref[...]
    softplus = jnp.maximum(xs, 0.0) + jnp.log1p(jnp.exp(-jnp.abs(xs)))
    gb = jnp.where(lane < 2 * DN_HEADS, _sigmoid(m), -decay_a_ref[...] * softplus)
    gb_ref[...] = jnp.where(lane < 4 * DN_HEADS, gb, 0.0)


def _dn_prep(qkv, misc, conv_w, a_log, dt_bias, bsz, seq):
    blk = SEQ_TILE
    nb = seq // blk
    qkv3 = qkv.reshape(bsz, seq, 3 * GROUP_WIDTH)
    misc3 = misc.reshape(bsz, seq, LANES)
    prev, main, nxt = _halo_specs(blk, 3 * GROUP_WIDTH, nb)
    convw = jnp.zeros((HALO, 3 * GROUP_WIDTH), F32).at[:DN_CONV].set(conv_w.astype(F32))
    lane_vec = lambda v: jnp.zeros((1, LANES), F32).at[0, 2 * DN_HEADS:4 * DN_HEADS].set(v.reshape(-1).astype(F32))
    decay_a = lane_vec(jnp.exp(a_log.astype(F32)))
    dtb = lane_vec(dt_bias)
    head = np.arange(GROUP_WIDTH) // DN_DK
    headsum = jnp.asarray((head[:, None] == head[None, :]).astype(np.float32))
    fixed = lambda b, i: (0, 0)
    qkvn, gb = pl.pallas_call(
        _dn_prep_kernel,
        grid=(bsz, nb),
        in_specs=[prev, main, nxt,
                  pl.BlockSpec((None, blk, LANES), lambda b, i: (b, i, 0)),
                  pl.BlockSpec((HALO, 3 * GROUP_WIDTH), fixed),
                  pl.BlockSpec((1, LANES), fixed),
                  pl.BlockSpec((1, LANES), fixed),
                  pl.BlockSpec((GROUP_WIDTH, GROUP_WIDTH), fixed)],
        out_specs=[pl.BlockSpec((None, blk, 3 * GROUP_WIDTH), lambda b, i: (b, i, 0)),
                   pl.BlockSpec((None, blk, LANES), lambda b, i: (b, i, 0))],
        out_shape=[jax.ShapeDtypeStruct((bsz, seq, 3 * GROUP_WIDTH), BF16),
                   jax.ShapeDtypeStruct((bsz, seq, LANES), F32)],
        compiler_params=_cparams(("parallel", "parallel")),
        name="dn_prep",
    )(qkv3, qkv3, qkv3, misc3, convw, decay_a, dtb, headsum)
    return qkvn, gb


def _unit_triangular_inverses(mats, rows, cols, lowers):
    c = mats[0].shape[0]
    eye = (rows == cols).astype(F32)

    def level_mask(lower, s):
        hi, lo = (rows, cols) if lower else (cols, rows)
        return ((hi // s) == (lo // s) + 1) & ((hi // (2 * s)) == (lo // (2 * s)))

    ts = [eye - jnp.where(level_mask(lower, 1), a, 0.0) for a, lower in zip(mats, lowers)]
    s = 2
    while s < c:
        tc = [_dot(t, jnp.where(level_mask(lower, s), a, 0.0)) for t, a, lower in zip(ts, mats, lowers)]
        ts = [t - _dot(x, t) for t, x in zip(ts, tc)]
        s *= 2
    return ts


def _dn_main_kernel(qkv_f_ref, gb_f_ref, qkv_b_ref, gb_b_ref, o_f_ref, o_b_ref, state_ref):
    blk = qkv_f_ref.shape[0]
    nc = blk // DN_CHUNK
    c_ = DN_CHUNK

    @pl.when(pl.program_id(1) == 0)
    def _():
        state_ref[...] = jnp.zeros_like(state_ref)

    rows = lax.broadcasted_iota(jnp.int32, (c_, c_), 0)
    cols = lax.broadcasted_iota(jnp.int32, (c_, c_), 1)
    tri_lo = (rows >= cols).astype(F32)
    tri_up = (rows <= cols).astype(F32)
    n_rec = 2 * DN_HEADS
    lowers = [i < DN_HEADS for i in range(n_rec)]

    def transpose_chunk(m):
        return jnp.concatenate([m, jnp.zeros((LANES - c_, LANES), F32)], axis=0).T[:, :c_]

    lane = lax.broadcasted_iota(jnp.int32, (c_, LANES), 1)

    def group_step(gi, carry):
        r0s, qs, qds, ks, vs, kdts, gams, b_rows, gam_rows, e_tots = [], [], [], [], [], [], [], [], [], []
        for g in range(DN_GROUP):
            ci = gi * DN_GROUP + g
            for d in range(2):
                lower = d == 0
                qkv_ref, gb_ref = (qkv_f_ref, gb_f_ref) if lower else (qkv_b_ref, gb_b_ref)
                r0 = pl.multiple_of((ci if lower else nc - 1 - ci) * c_, c_)
                r0s.append(r0)
                x = qkv_ref[pl.ds(r0, c_), :].astype(F32)
                gbt = gb_ref[pl.ds(r0, c_), :]
                incl = (rows >= cols) if lower else (rows <= cols)
                gc = jnp.dot(tri_lo if lower else tri_up, gbt, precision=lax.Precision.HIGHEST,
                             preferred_element_type=F32)
                stats_t = transpose_chunk(jnp.where(lane < n_rec, gbt, gc))
                beta_t = stats_t[0:n_rec, :]
                gc_t = stats_t[n_rec:2 * n_rec, :]
                g_last_t = gc_t[:, c_ - 1:c_] if lower else gc_t[:, 0:1]
                e_gc_t = jnp.exp(gc_t)
                e_rest_t = jnp.exp(g_last_t - gc_t)
                e_tot = jnp.exp(gc[c_ - 1:c_, :] if lower else gc[0:1, :])
                k_t = [transpose_chunk(x[:, GROUP_WIDTH + j * LANES:GROUP_WIDTH + (j + 1) * LANES])
                       for j in range(GROUP_WIDTH // LANES)]
                for h in range(DN_HEADS):
                    lb = DN_HEADS * d + h
                    lg = n_rec + lb
                    b_row = beta_t[lb:lb + 1, :]
                    gc_col = jnp.broadcast_to(gc[:, lg:lg + 1], (c_, c_))
                    diff = gc_col - gc_t[lb:lb + 1, :]
                    half = (h * DN_DK) % LANES
                    qs.append(x[:, h * DN_DK:(h + 1) * DN_DK])
                    qds.append(qs[-1] * jnp.exp(gc_col))
                    ks.append(x[:, GROUP_WIDTH + h * DN_DK:GROUP_WIDTH + (h + 1) * DN_DK])
                    vs.append(x[:, 2 * GROUP_WIDTH + h * DN_DK:2 * GROUP_WIDTH + (h + 1) * DN_DK])
                    kdts.append(k_t[(h * DN_DK) // LANES][half:half + DN_DK, :] * (e_rest_t[lb:lb + 1, :] * b_row))
                    gams.append(jnp.where(incl, jnp.exp(jnp.where(incl, diff, 0.0)), 0.0))
                    b_rows.append(b_row)
                    gam_rows.append(e_gc_t[lb:lb + 1, :])
                    e_tots.append(e_tot[:, lg:lg + 1])
        every = range(DN_GROUP * n_rec)
        low = [lowers[i % n_rec] for i in every]
        gkk = [_dot_nt(ks[i], ks[i]) for i in every]
        gqk = [_dot_nt(qs[i], ks[i]) for i in every]
        a = [jnp.where((rows > cols) if low[i] else (rows < cols), gkk[i] * gams[i], 0.0) * b_rows[i] for i in every]
        qk = [gqk[i] * gams[i] * b_rows[i] for i in every]
        t = _unit_triangular_inverses(a, rows, cols, low)
        us = [_dot(t[i], vs[i]) for i in every]
        ws = [_dot(t[i] * gam_rows[i], ks[i]) for i in every]
        states = [state_ref[r] for r in range(n_rec)]
        for g in range(DN_GROUP):
            rec = range(g * n_rec, (g + 1) * n_rec)
            qws = [_dot(jnp.concatenate([qds[i], ws[i]], axis=0), states[i % n_rec]) for i in rec]
            v_new = [us[i] - x[c_:, :] for i, x in zip(rec, qws)]
            outs = [x[:c_, :] + _dot(qk[i], vn) for i, x, vn in zip(rec, qws, v_new)]
            upd = [_dot(kdts[i], vn) for i, vn in zip(rec, v_new)]
            states = [states[i % n_rec] * e_tots[i] + u for i, u in zip(rec, upd)]
            o_f_ref[pl.ds(r0s[2 * g], c_), :] = jnp.concatenate(outs[:DN_HEADS], axis=1)
            o_b_ref[pl.ds(r0s[2 * g + 1], c_), :] = jnp.concatenate(outs[DN_HEADS:], axis=1)
        for r in range(n_rec):
            state_ref[r] = states[r]
        return carry

    lax.fori_loop(0, nc // DN_GROUP, group_step, 0)


def _dn_main(qkvn, gb, bsz, seq):
    blk = SEQ_TILE
    nb = seq // blk
    fwd = lambda b, i: (b, i, 0)
    bwd = lambda b, i: (b, nb - 1 - i, 0)
    return pl.pallas_call(
        _dn_main_kernel,
        grid=(bsz, nb),
        in_specs=[pl.BlockSpec((None, blk, 3 * GROUP_WIDTH), fwd),
                  pl.BlockSpec((None, blk, LANES), fwd),
                  pl.BlockSpec((None, blk, 3 * GROUP_WIDTH), bwd),
                  pl.BlockSpec((None, blk, LANES), bwd)],
        out_specs=[pl.BlockSpec((None, blk, GROUP_WIDTH), fwd),
                   pl.BlockSpec((None, blk, GROUP_WIDTH), bwd)],
        out_shape=[jax.ShapeDtypeStruct((bsz, seq, GROUP_WIDTH), F32)] * 2,
        scratch_shapes=[pltpu.VMEM((2 * DN_HEADS, DN_DK, DN_DK), F32)],
        compiler_params=_cparams(("parallel", "arbitrary")),
        name="dn_main",
    )(qkvn, gb, qkvn, gb)


def _sgu_kernel(x_ref, g_ref, b_ref, w_ref, bias_ref, o_ref):
    blk = x_ref.shape[0]
    x = x_ref[...].astype(F32)
    x = 0.5 * x * (1.0 + jnp.tanh(np.sqrt(2.0 / np.pi) * (x + 0.044715 * x * x * x)))
    u = x[:, :SGU_WIDTH]
    v = _layer_norm(x[:, SGU_WIDTH:], g_ref[...], b_ref[...]).astype(BF16)
    group = lax.broadcasted_iota(jnp.int32, (SGU_CHUNK, SGU_WIDTH), 1) // (SGU_WIDTH // SGU_GROUPS)
    for c in range(blk // SGU_CHUNK):
        r = slice(c * SGU_CHUNK, (c + 1) * SGU_CHUNK)
        vc = v[r, :]
        mixed = bias_ref[...]
        for gi in range(SGU_GROUPS):
            mixed = mixed + jnp.where(group == gi, _dot(w_ref[gi], vc), 0.0)
        o_ref[r, :] = (u[r, :] * mixed).astype(BF16)


def _sgu(x, ln_g, ln_b, w_s, b_s, bsz, seq):
    t = x.shape[0]
    blk = SEQ_TILE
    bias = jnp.repeat(b_s.T.astype(F32), SGU_WIDTH // SGU_GROUPS, axis=1)
    fixed2 = lambda i: (0, 0)
    return pl.pallas_call(
        _sgu_kernel,
        grid=(t // blk,),
        in_specs=[pl.BlockSpec((blk, 2 * SGU_WIDTH), lambda i: (i, 0)),
                  pl.BlockSpec((1, SGU_WIDTH), fixed2),
                  pl.BlockSpec((1, SGU_WIDTH), fixed2),
                  pl.BlockSpec((SGU_GROUPS, SGU_CHUNK, SGU_CHUNK), lambda i: (0, 0, 0)),
                  pl.BlockSpec((SGU_CHUNK, SGU_WIDTH), fixed2)],
        out_specs=pl.BlockSpec((blk, SGU_WIDTH), lambda i: (i, 0)),
        out_shape=jax.ShapeDtypeStruct((t, SGU_WIDTH), BF16),
        compiler_params=_cparams(("parallel",)),
        name="sgu",
    )(x, ln_g.reshape(1, -1).astype(F32), ln_b.reshape(1, -1).astype(F32), w_s.astype(BF16), bias)


def _pool_kernel(seq, prev_ref, main_ref, next_ref, w_ref, scale_ref, o_ref):
    blk = main_ref.shape[0]
    n = blk + 2 * HALO
    xh = _with_halo(prev_ref, main_ref, next_ref)
    shift = lambda a, k: pltpu.roll(a, k % n, 0)
    s2 = shift(xh, 1) + xh
    s4 = shift(s2, 1) + shift(s2, -1)
    s8 = shift(s4, 2) + shift(s4, -2)
    s16 = shift(s8, 4) + shift(s8, -4)
    sums = (s2, s4, s8, s16)
    x = xh[HALO:HALO + blk, :]
    group = lax.broadcasted_iota(jnp.int32, (blk, GROUP_WIDTH), 1) // POOL_GW
    pos = lax.broadcasted_iota(jnp.int32, (blk, GROUP_WIDTH), 0) + pl.program_id(1) * blk
    wsum = jnp.zeros((blk, GROUP_WIDTH), F32)
    cnt = jnp.ones((blk, GROUP_WIDTH), F32)
    for gi, win in enumerate(POOL_WINDOWS):
        hi = jnp.minimum(pos + win // 2, seq)
        lo = jnp.maximum(pos - win // 2, 0)
        wsum = jnp.where(group == gi, sums[gi][HALO:HALO + blk, :], wsum)
        cnt = jnp.where(group == gi, (hi - lo).astype(F32), cnt)
    pooled = wsum / cnt - x
    o_ref[...] = (_dot(pooled, w_ref[...]) * scale_ref[...]).astype(BF16)


def _pool(x, w_pool, scale, bsz, seq):
    blk = SEQ_TILE
    nb = seq // blk
    x3 = x.reshape(bsz, seq, GROUP_WIDTH)
    prev, main, nxt = _halo_specs(blk, GROUP_WIDTH, nb)
    wbd = jax.scipy.linalg.block_diag(*[w_pool[g].astype(F32) for g in range(len(POOL_WINDOWS))])
    fixed = lambda b, i: (0, 0)
    out = pl.pallas_call(
        functools.partial(_pool_kernel, seq),
        grid=(bsz, nb),
        in_specs=[prev, main, nxt,
                  pl.BlockSpec((GROUP_WIDTH, GROUP_WIDTH), fixed),
                  pl.BlockSpec((1, GROUP_WIDTH), fixed)],
        out_specs=pl.BlockSpec((None, blk, GROUP_WIDTH), lambda b, i: (b, i, 0)),
        out_shape=jax.ShapeDtypeStruct((bsz, seq, GROUP_WIDTH), BF16),
        compiler_params=_cparams(("parallel", "parallel")),
        name="pool",
    )(x3, x3, x3, wbd, scale.reshape(1, -1).astype(F32))
    return out.reshape(bsz * seq, GROUP_WIDTH)


def _rms(x, g):
    return x * lax.rsqrt(jnp.mean(x * x, axis=-1, keepdims=True) + RMS_EPS) * g


def _mla_prep_kernel(x_ref, qg_ref, kvg_ref, wq_ref, wk_ref, wvt_ref, cos_ref, sin_ref, q_ref, k_ref, vt_ref):
    hw = MLA_HEADS * LANES
    cq = _rms(x_ref[:, 0:MLA_Q_RANK].astype(F32), qg_ref[...]).astype(BF16)
    ckv = _rms(x_ref[:, MLA_Q_RANK:MLA_Q_RANK + MLA_KV_RANK].astype(F32), kvg_ref[...]).astype(BF16)
    c0 = MLA_Q_RANK + MLA_KV_RANK
    cos = cos_ref[...]
    sin = sin_ref[...]
    k_rope = (x_ref[:, c0:c0 + LANES].astype(F32) * cos
              + x_ref[:, c0 + LANES:c0 + 2 * LANES].astype(F32) * sin)
    qa = _dot(cq, wq_ref[:, 0:hw])
    qb = _dot(cq, wq_ref[:, hw:2 * hw])
    kn = _dot(ckv, wk_ref[...])
    vt = _dot_nt(wvt_ref[...], ckv).astype(BF16)
    ones = jnp.ones((MLA_VROWS - MLA_V, vt.shape[1]), BF16)
    vt = jnp.concatenate([piece for h in range(MLA_HEADS)
                          for piece in (vt[h * MLA_V:(h + 1) * MLA_V, :], ones)], axis=0)
    for c in range(vt_ref.shape[0]):
        vt_ref[c] = vt[:, c * ATT_TK:(c + 1) * ATT_TK]
    for h in range(MLA_HEADS):
        c = slice(h * LANES, (h + 1) * LANES)
        q_ref[:, c] = (qa[:, c] * cos + qb[:, c] * sin).astype(BF16)
        k_ref[:, c] = (kn[:, c] + k_rope).astype(BF16)


def _rope_tables(seq):
    inv = ROPE_THETA ** (-jnp.arange(0, MLA_ROPE, 2, dtype=F32) / MLA_ROPE)
    ang = jnp.arange(seq, dtype=F32)[:, None] * inv[None, :]
    pad = LANES - MLA_NOPE - MLA_ROPE
    cos = jnp.concatenate([jnp.ones((seq, MLA_NOPE), F32), jnp.cos(ang), jnp.cos(ang), jnp.zeros((seq, pad), F32)], 1)
    sin = jnp.concatenate([jnp.zeros((seq, MLA_NOPE), F32), jnp.sin(ang), jnp.sin(ang), jnp.zeros((seq, pad), F32)], 1)
    return cos, sin


def _pack_mla_weights(w_uq, w_uk, w_uv):
    half = MLA_ROPE // 2
    scale = (MLA_NOPE + MLA_ROPE) ** -0.5 * np.log2(np.e)
    pad = LANES - MLA_NOPE - MLA_ROPE
    qa, qb, kn = [], [], []
    for h in range(MLA_HEADS):
        wq = w_uq[:, h, :].astype(F32) * scale
        rope = wq[:, MLA_NOPE:]
        rot = jnp.concatenate([-rope[:, half:], rope[:, :half]], axis=1)
        zq = lambda n: jnp.zeros((MLA_Q_RANK, n), F32)
        qa.append(jnp.concatenate([wq, zq(pad)], axis=1))
        qb.append(jnp.concatenate([zq(MLA_NOPE), rot, zq(pad)], axis=1))
        kn.append(jnp.concatenate([w_uk[:, h, :].astype(F32), jnp.zeros((MLA_KV_RANK, LANES - MLA_NOPE), F32)], axis=1))
    wq_packed = jnp.concatenate(qa + qb, axis=1).astype(BF16)
    wk_packed = jnp.concatenate(kn, axis=1).astype(BF16)
    wv_t = w_uv.reshape(MLA_KV_RANK, MLA_HEADS * MLA_V).T.astype(BF16)
    return wq_packed, wk_packed, wv_t


def _mla_prep(x, q_g, kv_g, wq_packed, wk_packed, wv_t, cos, sin, bsz, seq):
    t = x.shape[0]
    tm = ROW_TILE
    per_seq = seq // tm
    per_tile = tm // ATT_TK
    hw = MLA_HEADS * LANES
    row = lambda i: (i, 0)
    fixed = lambda i: (0, 0)
    table = lambda i: (i % per_seq, 0)
    return pl.pallas_call(
        _mla_prep_kernel,
        grid=(t // tm,),
        in_specs=[pl.BlockSpec((tm, 768), row),
                  pl.BlockSpec((1, MLA_Q_RANK), fixed),
                  pl.BlockSpec((1, MLA_KV_RANK), fixed),
                  pl.BlockSpec((MLA_Q_RANK, 2 * hw), fixed),
                  pl.BlockSpec((MLA_KV_RANK, hw), fixed),
                  pl.BlockSpec((GROUP_WIDTH, MLA_KV_RANK), fixed),
                  pl.BlockSpec((tm, LANES), table),
                  pl.BlockSpec((tm, LANES), table)],
        out_specs=[pl.BlockSpec((tm, hw), row),
                   pl.BlockSpec((tm, hw), row),
                   pl.BlockSpec((None, per_tile, MLA_HEADS * MLA_VROWS, ATT_TK),
                                lambda i: (i // per_seq, i % per_seq, 0, 0))],
        out_shape=[jax.ShapeDtypeStruct((t, hw), BF16),
                   jax.ShapeDtypeStruct((t, hw), BF16),
                   jax.ShapeDtypeStruct((bsz, seq // ATT_TK, MLA_HEADS * MLA_VROWS, ATT_TK), BF16)],
        compiler_params=_cparams(("parallel",)),
        name="mla_prep",
    )(x, q_g.reshape(1, -1).astype(F32), kv_g.reshape(1, -1).astype(F32), wq_packed, wk_packed, wv_t, cos, sin)


def _attn_kernel(q_ref, k_ref, vt_ref, o_ref):
    n_steps = vt_ref.shape[0]
    tq = q_ref.shape[0]
    tk = ATT_TK
    unroll = min(n_steps, ATT_UNROLL)
    qs = [q_ref[:, h * LANES:(h + 1) * LANES] for h in range(2)]

    def scores(h, j):
        r0 = pl.multiple_of(j * tk, tk)
        return _dot_nt(k_ref[pl.ds(r0, tk), h * LANES:(h + 1) * LANES], qs[h])

    def update(h, j, s, m, acc):
        m_new = jnp.maximum(m, jnp.max(s, axis=0, keepdims=True))
        p = jnp.exp2(s - m_new).astype(BF16)
        pv = _dot(vt_ref[j, h * MLA_VROWS:(h + 1) * MLA_VROWS, :], p)
        return m_new, acc * jnp.exp2(m - m_new) + pv

    def group(g, carry):
        base = g * unroll
        state = [list(carry[0:2]), list(carry[2:4])]
        pending = [[scores(h, base + u) for h in range(2)] for u in range(min(ATT_LOOKAHEAD, unroll))]
        for u in range(unroll):
            s_cur = pending.pop(0)
            if u + ATT_LOOKAHEAD < unroll:
                pending.append([scores(h, base + u + ATT_LOOKAHEAD) for h in range(2)])
            for h in range(2):
                state[h] = list(update(h, base + u, s_cur[h], *state[h]))
        return tuple(state[0] + state[1])

    one = (jnp.full((1, tq), -jnp.inf, F32), jnp.zeros((MLA_VROWS, tq), F32))
    _, acc0, _, acc1 = lax.fori_loop(0, n_steps // unroll, group, one + one)
    out_t = jnp.concatenate([acc[:MLA_V, :] / acc[MLA_V:MLA_V + 1, :] for acc in (acc0, acc1)], axis=0)
    o_ref[...] = out_t.T.astype(BF16)


def _attention(q, k, vt, bsz, seq):
    hw = MLA_HEADS * LANES
    q3, k3 = (a.reshape(bsz, seq, hw) for a in (q, k))
    tq = ATT_TQ
    out = pl.pallas_call(
        _attn_kernel,
        grid=(bsz, 2, seq // tq),
        in_specs=[pl.BlockSpec((None, tq, 2 * LANES), lambda b, p, i: (b, i, p)),
                  pl.BlockSpec((None, seq, 2 * LANES), lambda b, p, i: (b, 0, p)),
                  pl.BlockSpec((None, seq // ATT_TK, 2 * MLA_VROWS, ATT_TK), lambda b, p, i: (b, 0, p, 0))],
        out_specs=pl.BlockSpec((None, tq, LANES), lambda b, p, i: (b, i, p)),
        out_shape=jax.ShapeDtypeStruct((bsz, seq, GROUP_WIDTH), BF16),
        compiler_params=_cparams(("parallel", "parallel", "arbitrary"), 48),
        name="mla_attention",
    )(q3, k3, vt)
    return out.reshape(bsz * seq, GROUP_WIDTH)


def _out_proj_kernel(pre_ln, x_ref, lng_ref, lnb_ref, of_ref, ob_ref, z_ref, ng_ref, headmean_ref,
                     yb_ref, yc_ref, yd_ref, w_ref, g_ref, b_ref, o_ref):
    x = x_ref[...]
    if pre_ln:
        x = _layer_norm(x, lng_ref[...], lnb_ref[...])
    o = of_ref[...] + ob_ref[...]
    z = z_ref[...].astype(F32)
    ya = o * lax.rsqrt(_dot(o * o, headmean_ref[...]) + RMS_EPS) * ng_ref[...] * _silu(z)
    gw = GROUP_WIDTH
    mix = (_dot(ya.astype(BF16), w_ref[0:gw, :]) + _dot(yb_ref[...], w_ref[gw:2 * gw, :])
           + _dot(yc_ref[...], w_ref[2 * gw:3 * gw, :]) + _dot(yd_ref[...], w_ref[3 * gw:, :]))
    o_ref[...] = _layer_norm(DEEPNORM_ALPHA * x + mix, g_ref[...], b_ref[...])


def _out_proj(x, ln_in_g, ln_in_b, o_f, o_b, z, norm_g, y_b, y_c, y_d, w_out, ln_g, ln_b, pre_ln):
    t = x.shape[0]
    tm = ROW_TILE
    head = np.arange(GROUP_WIDTH) // DN_DK
    headmean = jnp.asarray((head[:, None] == head[None, :]).astype(np.float32) / DN_DK)
    ng = jnp.tile(norm_g.astype(F32), DN_HEADS).reshape(1, GROUP_WIDTH)
    row = lambda i: (i, 0)
    fixed = lambda i: (0, 0)
    quarter = pl.BlockSpec((tm, GROUP_WIDTH), row)
    vec = pl.BlockSpec((1, D_MODEL), fixed)
    return pl.pallas_call(
        functools.partial(_out_proj_kernel, pre_ln),
        grid=(t // tm,),
        in_specs=[pl.BlockSpec((tm, D_MODEL), row), vec, vec,
                  quarter, quarter, quarter,
                  pl.BlockSpec((1, GROUP_WIDTH), fixed),
                  pl.BlockSpec((GROUP_WIDTH, GROUP_WIDTH), fixed),
                  quarter, quarter, quarter,
                  pl.BlockSpec((D_MODEL, D_MODEL), fixed), vec, vec],
        out_specs=pl.BlockSpec((tm, D_MODEL), row),
        out_shape=jax.ShapeDtypeStruct((t, D_MODEL), F32),
        compiler_params=_cparams(("parallel",)),
        name="out_proj",
    )(x, ln_in_g, ln_in_b, o_f, o_b, z, ng, headmean, y_b, y_c, y_d, w_out.astype(BF16),
      ln_g.reshape(1, -1).astype(F32), ln_b.reshape(1, -1).astype(F32))


def _router_kernel(x_ref, whi_ref, wlo_ref, bias_ref, cls_ref, rank_ref, cnt_ref, carry_ref):
    tm = x_ref.shape[0]

    @pl.when(pl.program_id(0) == 0)
    def _():
        carry_ref[...] = jnp.zeros_like(carry_ref)

    x = x_ref[...]
    x_hi = x.astype(BF16)
    x_lo = (x - x_hi.astype(F32)).astype(BF16)
    logits = _dot_nt(whi_ref[...], x_hi) + (_dot_nt(whi_ref[...], x_lo) + _dot_nt(wlo_ref[...], x_hi))
    biased = _sigmoid(logits) + bias_ref[...]
    v = [biased[e:e + 1, :] for e in range(N_EXPERTS)]
    best = chosen = pair = None
    for g in range(N_EXPERT_GROUPS):
        m = v[g * EXPERTS_PER_GROUP:(g + 1) * EXPERTS_PER_GROUP]
        score = functools.reduce(jnp.maximum, [m[i] + m[j] for i, j in PAIRS])
        picked = []
        for e in range(EXPERTS_PER_GROUP):
            outranked = [(m[o] >= m[e]) if o < e else (m[o] > m[e]) for o in range(EXPERTS_PER_GROUP) if o != e]
            picked.append(functools.reduce(jnp.add, [c.astype(jnp.int32) for c in outranked]) < 2)
        first = jnp.where(picked[0], 0, jnp.where(picked[1], 1, 2))
        last = jnp.where(picked[3], 3, jnp.where(picked[2], 2, 1))
        pair_g = (first * (7 - first)) // 2 + (last - first - 1)
        if g == 0:
            best, chosen, pair = score, jnp.zeros_like(pair_g), pair_g
        else:
            better = score > best
            best = jnp.where(better, score, best)
            chosen = jnp.where(better, g, chosen)
            pair = jnp.where(better, pair_g, pair)
    cls_row = (chosen * len(PAIRS) + pair).astype(F32)
    onehot = (lax.broadcasted_iota(jnp.int32, (CLASS_ROWS, tm), 0).astype(F32) == cls_row)
    upto = (lax.broadcasted_iota(jnp.int32, (tm, tm), 0) <= lax.broadcasted_iota(jnp.int32, (tm, tm), 1))
    prefix = _dot(onehot.astype(BF16), upto.astype(BF16))
    carry = carry_ref[:, 0:1]
    rank = jnp.sum(jnp.where(onehot, prefix + carry, 0.0), axis=0, keepdims=True) - 1.0
    cls_ref[...] = jnp.broadcast_to(cls_row, (HALO, tm))
    rank_ref[...] = jnp.broadcast_to(rank, (HALO, tm))
    total = carry + prefix[:, tm - 1:tm]
    carry_ref[...] = jnp.broadcast_to(total, carry_ref.shape)
    cnt_ref[...] = jnp.broadcast_to(total, cnt_ref.shape)


def _router(x, router_w, router_bias):
    t = x.shape[0]
    tm = ROW_TILE
    w = router_w.astype(F32).T
    w_hi = w.astype(BF16)
    w_lo = (w - w_hi.astype(F32)).astype(BF16)
    bias = router_bias.astype(F32).reshape(N_EXPERTS, 1)
    fixed = lambda i: (0, 0)
    cls, rank, cnt = pl.pallas_call(
        _router_kernel,
        grid=(t // tm,),
        in_specs=[pl.BlockSpec((tm, D_MODEL), lambda i: (i, 0)),
                  pl.BlockSpec((N_EXPERTS, D_MODEL), fixed),
                  pl.BlockSpec((N_EXPERTS, D_MODEL), fixed),
                  pl.BlockSpec((N_EXPERTS, 1), fixed)],
        out_specs=[pl.BlockSpec((HALO, tm), lambda i: (0, i)),
                   pl.BlockSpec((HALO, tm), lambda i: (0, i)),
                   pl.BlockSpec((CLASS_ROWS, LANES), fixed)],
        out_shape=[jax.ShapeDtypeStruct((HALO, t), F32),
                   jax.ShapeDtypeStruct((HALO, t), F32),
                   jax.ShapeDtypeStruct((CLASS_ROWS, LANES), F32)],
        scratch_shapes=[pltpu.VMEM((CLASS_ROWS, LANES), F32)],
        compiler_params=_cparams(("arbitrary",)),
        name="router",
    )(x, w_hi, w_lo, bias)
    return cls[0].astype(jnp.int32), rank[0].astype(jnp.int32), cnt[:N_CLASSES, 0].astype(jnp.int32)


def _routing_tables(cls, rank, counts, n_tiles):
    tm = MOE_TILE
    tiles = (counts + tm - 1) // tm
    first_tile = jnp.cumsum(tiles) - tiles
    dest = first_tile[cls] * tm + rank
    tile_ids = jnp.arange(n_tiles, dtype=jnp.int32)
    tile_cls = jnp.sum((jnp.cumsum(tiles)[None, :] <= tile_ids[:, None]).astype(jnp.int32), axis=1)
    tile_cls = jnp.minimum(tile_cls, N_CLASSES - 1)
    used = (tile_ids < jnp.sum(tiles)).astype(jnp.int32)
    pair = np.asarray(PAIRS, np.int32)
    base = (tile_cls // len(PAIRS)) * EXPERTS_PER_GROUP
    e1 = base + jnp.asarray(pair[:, 0])[tile_cls % len(PAIRS)]
    e2 = base + jnp.asarray(pair[:, 1])[tile_cls % len(PAIRS)]
    return dest.astype(jnp.int32), e1.astype(jnp.int32), e2.astype(jnp.int32), used


def _dispatch_kernel(dest_ref, x_ref, zeros_ref, xs_ref, sem):
    del zeros_ref
    tr = x_ref.shape[0]

    def row_copy(r):
        return pltpu.make_async_copy(x_ref.at[pl.ds(r, 1), :], xs_ref.at[pl.ds(dest_ref[0, 0, r], 1), :], sem)

    for r in range(tr):
        row_copy(r).start()
    for r in range(tr):
        row_copy(r).wait()


def _dispatch(x, dest, n_slots):
    t = x.shape[0]
    tr = DMA_TILE
    dest3 = dest.reshape(t // tr, 1, tr)
    zeros = jnp.zeros((n_slots, D_MODEL), F32)
    return pl.pallas_call(
        _dispatch_kernel,
        grid=(t // tr,),
        in_specs=[pl.BlockSpec((1, 1, tr), lambda i: (i, 0, 0), memory_space=pltpu.SMEM),
                  pl.BlockSpec((tr, D_MODEL), lambda i: (i, 0)),
                  pl.BlockSpec(memory_space=pl.ANY)],
        out_specs=pl.BlockSpec(memory_space=pl.ANY),
        out_shape=jax.ShapeDtypeStruct((n_slots, D_MODEL), F32),
        scratch_shapes=[pltpu.SemaphoreType.DMA],
        input_output_aliases={2: 0},
        compiler_params=_cparams(("arbitrary",)),
        name="moe_dispatch",
    )(dest3, x, zeros)


def _moe_kernel(e1_ref, e2_ref, used_ref, x_ref, rw_ref, wg1_ref, wu1_ref, wd1_ref, wg2_ref, wu2_ref, wd2_ref, y_ref):
    i = pl.program_id(0)

    @pl.when(used_ref[i] == 1)
    def _():
        xb = x_ref[...].astype(BF16)
        scores = _sigmoid(_dot(xb, rw_ref[...]))
        lane = lax.broadcasted_iota(jnp.int32, scores.shape, 1)
        s1 = jnp.sum(jnp.where(lane == e1_ref[i], scores, 0.0), axis=-1, keepdims=True)
        s2 = jnp.sum(jnp.where(lane == e2_ref[i], scores, 0.0), axis=-1, keepdims=True)
        inv = 1.0 / (s1 + s2)
        h1 = _silu(_dot(xb, wg1_ref[...])) * _dot(xb, wu1_ref[...]) * (s1 * inv)
        h2 = _silu(_dot(xb, wg2_ref[...])) * _dot(xb, wu2_ref[...]) * (s2 * inv)
        y_ref[...] = _dot(h1.astype(BF16), wd1_ref[...]) + _dot(h2.astype(BF16), wd2_ref[...])

    @pl.when(used_ref[i] == 0)
    def _():
        y_ref[...] = jnp.zeros_like(y_ref)


def _moe(xs, e1, e2, used, router_w, w_gate, w_up, w_down):
    n_slots = xs.shape[0]
    tm = MOE_TILE
    rw = jnp.zeros((D_MODEL, LANES), BF16).at[:, :N_EXPERTS].set(router_w.astype(BF16))
    up = lambda sel: pl.BlockSpec((None, D_MODEL, EXPERT_FF), lambda i, e1, e2, u: ((e1, e2)[sel][i], 0, 0))
    down = lambda sel: pl.BlockSpec((None, EXPERT_FF, D_MODEL), lambda i, e1, e2, u: ((e1, e2)[sel][i], 0, 0))
    grid_spec = pltpu.PrefetchScalarGridSpec(
        num_scalar_prefetch=3,
        grid=(n_slots // tm,),
        in_specs=[pl.BlockSpec((tm, D_MODEL), lambda i, e1, e2, u: (i, 0)),
                  pl.BlockSpec((D_MODEL, LANES), lambda i, e1, e2, u: (0, 0)),
                  up(0), up(0), down(0), up(1), up(1), down(1)],
        out_specs=pl.BlockSpec((tm, D_MODEL), lambda i, e1, e2, u: (i, 0)),
    )
    return pl.pallas_call(
        _moe_kernel,
        grid_spec=grid_spec,
        out_shape=jax.ShapeDtypeStruct((n_slots, D_MODEL), F32),
        compiler_params=_cparams(("arbitrary",), 48),
        name="moe_experts",
    )(e1, e2, used, xs, rw, w_gate, w_up, w_down, w_gate, w_up, w_down)


def _combine_kernel(dest_ref, dest_next_ref, x_ref, ys_ref, g_ref, b_ref, o_ref, buf, sem):
    tr = x_ref.shape[0]
    i = pl.program_id(0)
    slot = i % 2

    def row_copy(idx_ref, r, s):
        return pltpu.make_async_copy(ys_ref.at[pl.ds(idx_ref[0, 0, r], 1), :], buf.at[s, pl.ds(r, 1), :], sem.at[s])

    @pl.when(i == 0)
    def _():
        for r in range(tr):
            row_copy(dest_ref, r, slot).start()

    @pl.when(i + 1 < pl.num_programs(0))
    def _():
        for r in range(tr):
            row_copy(dest_next_ref, r, 1 - slot).start()

    for r in range(tr):
        row_copy(dest_ref, r, slot).wait()
    o_ref[...] = _layer_norm(DEEPNORM_ALPHA * x_ref[...] + buf[slot], g_ref[...], b_ref[...])


def _combine(x, ys, dest, ln_g, ln_b):
    t = x.shape[0]
    tr = DMA_TILE
    n = t // tr
    dest3 = dest.reshape(n, 1, tr)
    fixed = lambda i: (0, 0)
    return pl.pallas_call(
        _combine_kernel,
        grid=(n,),
        in_specs=[pl.BlockSpec((1, 1, tr), lambda i: (i, 0, 0), memory_space=pltpu.SMEM),
                  pl.BlockSpec((1, 1, tr), lambda i: (jnp.minimum(i + 1, n - 1), 0, 0), memory_space=pltpu.SMEM),
                  pl.BlockSpec((tr, D_MODEL), lambda i: (i, 0)),
                  pl.BlockSpec(memory_space=pl.ANY),
                  pl.BlockSpec((1, D_MODEL), fixed),
                  pl.BlockSpec((1, D_MODEL), fixed)],
        out_specs=pl.BlockSpec((tr, D_MODEL), lambda i: (i, 0)),
        out_shape=jax.ShapeDtypeStruct((t, D_MODEL), F32),
        scratch_shapes=[pltpu.VMEM((2, tr, D_MODEL), F32), pltpu.SemaphoreType.DMA((2,))],
        compiler_params=_cparams(("arbitrary",)),
        name="moe_combine",
    )(dest3, dest3, x, ys, ln_g.reshape(1, -1).astype(F32), ln_b.reshape(1, -1).astype(F32))


def _trunk(x, p):
    bsz, seq, d = x.shape
    t = bsz * seq
    x = x.reshape(t, d)
    cos, sin = _rope_tables(seq)
    n_tiles = t // MOE_TILE + N_CLASSES
    for l in range(DEPTH):
        first = l == 0
        qkv, z, sgu_in, pool_in, mla_in, misc = _in_proj(x, p["ln_in_g"], p["ln_in_b"], p["w_in"][l], first)
        qkvn, gb = _dn_prep(qkv, misc, p["dn_conv_w"][l], p["dn_a_log"][l], p["dn_dt_bias"][l], bsz, seq)
        o_f, o_b = _dn_main(qkvn, gb, bsz, seq)
        y_b = _sgu(sgu_in, p["sgu_ln_g"][l], p["sgu_ln_b"][l], p["sgu_w"][l], p["sgu_b"][l], bsz, seq)
        y_c = _pool(pool_in, p["pool_w"][l], p["pool_scale"][l], bsz, seq)
        q, k, vt = _mla_prep(mla_in, p["mla_q_norm_g"][l], p["mla_kv_norm_g"][l], p["wq"][l], p["wk"][l],
                             p["wvt"][l], cos, sin, bsz, seq)
        y_d = _attention(q, k, vt, bsz, seq)
        x = _out_proj(x, p["ln_in_g"], p["ln_in_b"], o_f.reshape(t, -1), o_b.reshape(t, -1), z,
                      p["dn_norm_g"][l], y_b, y_c, y_d, p["w_out"][l], p["ln1_g"][l], p["ln1_b"][l], first)
        cls, rank, counts = _router(x, p["router_w"], p["router_bias"])
        dest, e1, e2, used = _routing_tables(cls, rank, counts, n_tiles)
        xs = _dispatch(x, dest, n_tiles * MOE_TILE)
        ys = _moe(xs, e1, e2, used, p["router_w"], p["moe_w_gate"][l], p["moe_w_up"][l], p["moe_w_down"][l])
        x = _combine(x, ys, dest, p["ln2_g"][l], p["ln2_b"][l])
    return x.reshape(bsz, seq, d)


def kernel(x_prompt, x_sample, ln_in_g, ln_in_b, w_in, dn_conv_w, dn_a_log, dn_dt_bias, dn_norm_g, sgu_ln_g, sgu_ln_b, sgu_w, sgu_b, pool_w, pool_scale, mla_q_norm_g, mla_kv_norm_g, mla_w_uq, mla_w_uk, mla_w_uv, w_out, ln1_g, ln1_b, router_w, router_bias, moe_w_gate, moe_w_up, moe_w_down, ln2_g, ln2_b):
    packed_mla = [_pack_mla_weights(mla_w_uq[l], mla_w_uk[l], mla_w_uv[l]) for l in range(DEPTH)]
    p = dict(
        ln_in_g=ln_in_g.reshape(1, -1).astype(F32), ln_in_b=ln_in_b.reshape(1, -1).astype(F32),
        w_in=[_pack_in_weights(w_in[l]) for l in range(DEPTH)],
        dn_conv_w=dn_conv_w, dn_a_log=dn_a_log, dn_dt_bias=dn_dt_bias, dn_norm_g=dn_norm_g,
        sgu_ln_g=sgu_ln_g, sgu_ln_b=sgu_ln_b, sgu_w=sgu_w, sgu_b=sgu_b,
        pool_w=pool_w, pool_scale=pool_scale,
        mla_q_norm_g=mla_q_norm_g, mla_kv_norm_g=mla_kv_norm_g,
        wq=[w[0] for w in packed_mla], wk=[w[1] for w in packed_mla], wvt=[w[2] for w in packed_mla],
        w_out=w_out, ln1_g=ln1_g, ln1_b=ln1_b, router_w=router_w, router_bias=router_bias,
        moe_w_gate=moe_w_gate.astype(BF16), moe_w_up=moe_w_up.astype(BF16), moe_w_down=moe_w_down.astype(BF16),
        ln2_g=ln2_g, ln2_b=ln2_b,
    )
    return _trunk(x_prompt, p), _trunk(x_sample, p)
```

```python
import functools

import numpy as np
import jax
import jax.numpy as jnp
from jax import lax
from jax.experimental import pallas as pl
from jax.experimental.pallas import tpu as pltpu

F32 = jnp.float32
BF16 = jnp.bfloat16

D_MODEL = 1024
DEPTH = 2
GROUP_WIDTH = D_MODEL // 4
DN_HEADS = 4
DN_DK = GROUP_WIDTH // DN_HEADS
DN_CONV = 5
DN_CHUNK = 64
DN_GROUP = 4
SGU_GROUPS = 4
SGU_CHUNK = 128
SGU_WIDTH = GROUP_WIDTH
POOL_WINDOWS = (2, 4, 8, 16)
POOL_GW = GROUP_WIDTH // len(POOL_WINDOWS)
MLA_HEADS = 4
MLA_NOPE = 64
MLA_ROPE = 32
MLA_V = GROUP_WIDTH // MLA_HEADS
MLA_VROWS = MLA_V + 16
MLA_Q_RANK = 384
MLA_KV_RANK = 128
ROPE_THETA = 10000.0
N_EXPERTS = 16
N_EXPERT_GROUPS = 4
EXPERTS_PER_GROUP = N_EXPERTS // N_EXPERT_GROUPS
EXPERT_FF = 512
DEEPNORM_ALPHA = (2 * DEPTH) ** 0.25
LN_EPS = 1e-5
RMS_EPS = 1e-6

LANES = 128
HALO = 8

PAIRS = ((0, 1), (0, 2), (0, 3), (1, 2), (1, 3), (2, 3))
N_CLASSES = N_EXPERT_GROUPS * len(PAIRS)
CLASS_ROWS = 32

C_QKV = 0
C_Z = 768
C_SGU = 1024
C_POOL = 1536
C_MLA = 1792
C_MISC = 2560
IN_PACKED = 2688

ROW_TILE = 512
SEQ_TILE = 512
MOE_TILE = 256
DMA_TILE = 512
ATT_TQ = 256
ATT_TK = 256
ATT_UNROLL = 32
ATT_ALL_HEADS_MAX_STEPS = 8
ATT_LOOKAHEAD = 3


def _cparams(sem, vmem_mb=None):
    kw = dict(dimension_semantics=sem)
    if vmem_mb is not None:
        kw["vmem_limit_bytes"] = vmem_mb * 1024 * 1024
    return pltpu.CompilerParams(**kw)


def _layer_norm(x, g, b):
    mu = jnp.mean(x, axis=-1, keepdims=True)
    xc = x - mu
    var = jnp.mean(xc * xc, axis=-1, keepdims=True)
    return xc * lax.rsqrt(var + LN_EPS) * g + b


def _sigmoid(x):
    return 1.0 / (1.0 + jnp.exp(-x))


def _silu(x):
    return x * _sigmoid(x)


def _dot(a, b):
    return jnp.dot(a, b, preferred_element_type=F32)


def _dot_nt(a, b):
    return lax.dot_general(a, b, (((1,), (1,)), ((), ())), preferred_element_type=F32)


def _dot_tn(a, b):
    return lax.dot_general(a, b, (((0,), (0,)), ((), ())), preferred_element_type=F32)


def _in_proj_kernel(pre_ln, x_ref, g_ref, b_ref, w_ref,
                    qkv_ref, z_ref, sgu_ref, pool_ref, mla_ref, misc_ref):
    x = x_ref[...]
    if pre_ln:
        x = _layer_norm(x, g_ref[...], b_ref[...])
    xb = x.astype(BF16)
    qkv_ref[...] = _dot(xb, w_ref[:, C_QKV:C_Z]).astype(BF16)
    z_ref[...] = _dot(xb, w_ref[:, C_Z:C_SGU]).astype(BF16)
    sgu_ref[...] = _dot(xb, w_ref[:, C_SGU:C_POOL]).astype(BF16)
    pool_ref[...] = _dot(xb, w_ref[:, C_POOL:C_MLA]).astype(BF16)
    mla_ref[...] = _dot(xb, w_ref[:, C_MLA:C_MISC]).astype(BF16)
    misc_ref[...] = _dot(xb, w_ref[:, C_MISC:IN_PACKED])


def _in_proj(x, ln_g, ln_b, w_packed, pre_ln):
    t = x.shape[0]
    tm = ROW_TILE
    widths = (768, 256, 512, 256, 768, 128)
    dtypes = (BF16, BF16, BF16, BF16, BF16, F32)
    row = lambda i: (i, 0)
    fixed = lambda i: (0, 0)
    return pl.pallas_call(
        functools.partial(_in_proj_kernel, pre_ln),
        grid=(t // tm,),
        in_specs=[pl.BlockSpec((tm, D_MODEL), row),
                  pl.BlockSpec((1, D_MODEL), fixed),
                  pl.BlockSpec((1, D_MODEL), fixed),
                  pl.BlockSpec((D_MODEL, IN_PACKED), fixed)],
        out_specs=[pl.BlockSpec((tm, w), row) for w in widths],
        out_shape=[jax.ShapeDtypeStruct((t, w), d) for w, d in zip(widths, dtypes)],
        compiler_params=_cparams(("parallel",), 48),
        name="in_proj",
    )(x, ln_g, ln_b, w_packed)


def _pack_in_weights(w):
    d = w.shape[0]
    zeros = lambda n: jnp.zeros((d, n), w.dtype)
    qkv, z = w[:, 0:768], w[:, 768:1024]
    bb, aa = w[:, 1024:1032], w[:, 1032:1040]
    sgu, pool = w[:, 1040:1552], w[:, 1552:1808]
    cq, ckv, kr = w[:, 1808:2192], w[:, 2192:2320], w[:, 2320:2352]
    half = MLA_ROPE // 2
    kr_rot = jnp.concatenate([-kr[:, half:], kr[:, :half]], axis=1)
    rope_tile = jnp.concatenate([zeros(MLA_NOPE), kr, zeros(LANES - MLA_NOPE - MLA_ROPE)], axis=1)
    rot_tile = jnp.concatenate([zeros(MLA_NOPE), kr_rot, zeros(LANES - MLA_NOPE - MLA_ROPE)], axis=1)
    misc = jnp.concatenate([bb, aa, zeros(LANES - 16)], axis=1)
    return jnp.concatenate([qkv, z, sgu, pool, cq, ckv, rope_tile, rot_tile, misc], axis=1).astype(BF16)


def _halo_specs(blk, width, n_seq_blocks):
    per = blk // HALO
    last = n_seq_blocks * per - 1
    main = pl.BlockSpec((None, blk, width), lambda b, i: (b, i, 0))
    prev = pl.BlockSpec((None, HALO, width), lambda b, i: (b, jnp.maximum(i * per - 1, 0), 0))
    nxt = pl.BlockSpec((None, HALO, width), lambda b, i: (b, jnp.minimum((i + 1) * per, last), 0))
    return prev, main, nxt


def _with_halo(prev_ref, main_ref, next_ref):
    i = pl.program_id(1)
    n = pl.num_programs(1)
    prev = jnp.where(i > 0, prev_ref[...].astype(F32), 0.0)
    nxt = jnp.where(i < n - 1, next_ref[...].astype(F32), 0.0)
    return jnp.concatenate([prev, main_ref[...].astype(F32), nxt], axis=0)


def _dn_prep_kernel(prev_ref, main_ref, next_ref, misc_ref, convw_ref, decay_a_ref, dtb_ref, headsum_ref,
                    qkvn_ref, gb_ref):
    blk = main_ref.shape[0]
    xh = _with_halo(prev_ref, main_ref, next_ref)
    acc = jnp.zeros((blk, 3 * GROUP_WIDTH), F32)
    n = blk + 2 * HALO
    for j in range(DN_CONV):
        tap = xh if j == DN_CONV // 2 else pltpu.roll(xh, (DN_CONV // 2 - j) % n, 0)
        acc = acc + tap[HALO:HALO + blk, :] * convw_ref[j:j + 1, :]
    x = _silu(acc)
    q = x[:, 0:GROUP_WIDTH]
    k = x[:, GROUP_WIDTH:2 * GROUP_WIDTH]
    v = x[:, 2 * GROUP_WIDTH:]
    hs = headsum_ref[...]
    q = q * lax.rsqrt(_dot(q * q, hs) + RMS_EPS) * (DN_DK ** -0.5)
    k = k * lax.rsqrt(_dot(k * k, hs) + RMS_EPS)
    qkvn_ref[:, 0:GROUP_WIDTH] = q.astype(BF16)
    qkvn_ref[:, GROUP_WIDTH:2 * GROUP_WIDTH] = k.astype(BF16)
    qkvn_ref[:, 2 * GROUP_WIDTH:] = v.astype(BF16)
    m = misc_ref[...]
    lane = lax.broadcasted_iota(jnp.int32, m.shape, 1)
    xs = m + dtb_ref[...]
    softplus = jnp.maximum(xs, 0.0) + jnp.log1p(jnp.exp(-jnp.abs(xs)))
    gb = jnp.where(lane < 2 * DN_HEADS, _sigmoid(m), -decay_a_ref[...] * softplus)
    gb_ref[...] = jnp.where(lane < 4 * DN_HEADS, gb, 0.0)


def _dn_prep(qkv, misc, conv_w, a_log, dt_bias, bsz, seq):
    blk = SEQ_TILE
    nb = seq // blk
    qkv3 = qkv.reshape(bsz, seq, 3 * GROUP_WIDTH)
    misc3 = misc.reshape(bsz, seq, LANES)
    prev, main, nxt = _halo_specs(blk, 3 * GROUP_WIDTH, nb)
    convw = jnp.zeros((HALO, 3 * GROUP_WIDTH), F32).at[:DN_CONV].set(conv_w.astype(F32))
    lane_vec = lambda v: jnp.zeros((1, LANES), F32).at[0, 2 * DN_HEADS:4 * DN_HEADS].set(v.reshape(-1).astype(F32))
    decay_a = lane_vec(jnp.exp(a_log.astype(F32)))
    dtb = lane_vec(dt_bias)
    head = np.arange(GROUP_WIDTH) // DN_DK
    headsum = jnp.asarray((head[:, None] == head[None, :]).astype(np.float32))
    fixed = lambda b, i: (0, 0)
    qkvn, gb = pl.pallas_call(
        _dn_prep_kernel,
        grid=(bsz, nb),
        in_specs=[prev, main, nxt,
                  pl.BlockSpec((None, blk, LANES), lambda b, i: (b, i, 0)),
                  pl.BlockSpec((HALO, 3 * GROUP_WIDTH), fixed),
                  pl.BlockSpec((1, LANES), fixed),
                  pl.BlockSpec((1, LANES), fixed),
                  pl.BlockSpec((GROUP_WIDTH, GROUP_WIDTH), fixed)],
        out_specs=[pl.BlockSpec((None, blk, 3 * GROUP_WIDTH), lambda b, i: (b, i, 0)),
                   pl.BlockSpec((None, blk, LANES), lambda b, i: (b, i, 0))],
        out_shape=[jax.ShapeDtypeStruct((bsz, seq, 3 * GROUP_WIDTH), BF16),
                   jax.ShapeDtypeStruct((bsz, seq, LANES), F32)],
        compiler_params=_cparams(("parallel", "parallel")),
        name="dn_prep",
    )(qkv3, qkv3, qkv3, misc3, convw, decay_a, dtb, headsum)
    return qkvn, gb


def _unit_triangular_inverses(mats, rows, cols, lowers):
    c = mats[0].shape[0]
    eye = (rows == cols).astype(F32)

    def level_mask(lower, s):
        hi, lo = (rows, cols) if lower else (cols, rows)
        return ((hi // s) == (lo // s) + 1) & ((hi // (2 * s)) == (lo // (2 * s)))

    ts = [eye - jnp.where(level_mask(lower, 1), a, 0.0) for a, lower in zip(mats, lowers)]
    s = 2
    while s < c:
        tc = [_dot(t, jnp.where(level_mask(lower, s), a, 0.0)) for t, a, lower in zip(ts, mats, lowers)]
        ts = [t - _dot(x, t) for t, x in zip(ts, tc)]
        s *= 2
    return ts


def _dn_main_kernel(qkv_f_ref, gb_f_ref, qkv_b_ref, gb_b_ref, o_f_ref, o_b_ref, state_ref):
    blk = qkv_f_ref.shape[0]
    nc = blk // DN_CHUNK
    c_ = DN_CHUNK

    @pl.when(pl.program_id(1) == 0)
    def _():
        state_ref[...] = jnp.zeros_like(state_ref)

    rows = lax.broadcasted_iota(jnp.int32, (c_, c_), 0)
    cols = lax.broadcasted_iota(jnp.int32, (c_, c_), 1)
    tri_lo = (rows >= cols).astype(F32)
    tri_up = (rows <= cols).astype(F32)
    n_rec = 2 * DN_HEADS
    lowers = [i < DN_HEADS for i in range(n_rec)]

    def transpose_chunk(m):
        return jnp.concatenate([m, jnp.zeros((LANES - c_, LANES), F32)], axis=0).T[:, :c_]

    lane = lax.broadcasted_iota(jnp.int32, (c_, LANES), 1)

    def group_step(gi, carry):
        r0s, qs, qds, ks, vs, kdts, gams, b_rows, gam_rows, e_tots = [], [], [], [], [], [], [], [], [], []
        for g in range(DN_GROUP):
            ci = gi * DN_GROUP + g
            for d in range(2):
                lower = d == 0
                qkv_ref, gb_ref = (qkv_f_ref, gb_f_ref) if lower else (qkv_b_ref, gb_b_ref)
                r0 = pl.multiple_of((ci if lower else nc - 1 - ci) * c_, c_)
                r0s.append(r0)
                x = qkv_ref[pl.ds(r0, c_), :].astype(F32)
                gbt = gb_ref[pl.ds(r0, c_), :]
                incl = (rows >= cols) if lower else (rows <= cols)
                gc = jnp.dot(tri_lo if lower else tri_up, gbt, precision=lax.Precision.HIGHEST,
                             preferred_element_type=F32)
                stats_t = transpose_chunk(jnp.where(lane < n_rec, gbt, gc))
                beta_t = stats_t[0:n_rec, :]
                gc_t = stats_t[n_rec:2 * n_rec, :]
                g_last_t = gc_t[:, c_ - 1:c_] if lower else gc_t[:, 0:1]
                e_gc_t = jnp.exp(gc_t)
                e_rest_t = jnp.exp(g_last_t - gc_t)
                e_tot = jnp.exp(gc[c_ - 1:c_, :] if lower else gc[0:1, :])
                k_t = [transpose_chunk(x[:, GROUP_WIDTH + j * LANES:GROUP_WIDTH + (j + 1) * LANES])
                       for j in range(GROUP_WIDTH // LANES)]
                for h in range(DN_HEADS):
                    lb = DN_HEADS * d + h
                    lg = n_rec + lb
                    b_row = beta_t[lb:lb + 1, :]
                    gc_col = jnp.broadcast_to(gc[:, lg:lg + 1], (c_, c_))
                    diff = gc_col - gc_t[lb:lb + 1, :]
                    half = (h * DN_DK) % LANES
                    qs.append(x[:, h * DN_DK:(h + 1) * DN_DK])
                    qds.append(qs[-1] * jnp.exp(gc_col))
                    ks.append(x[:, GROUP_WIDTH + h * DN_DK:GROUP_WIDTH + (h + 1) * DN_DK])
                    vs.append(x[:, 2 * GROUP_WIDTH + h * DN_DK:2 * GROUP_WIDTH + (h + 1) * DN_DK])
                    kdts.append(k_t[(h * DN_DK) // LANES][half:half + DN_DK, :] * (e_rest_t[lb:lb + 1, :] * b_row))
                    gams.append(jnp.where(incl, jnp.exp(jnp.where(incl, diff, 0.0)), 0.0))
                    b_rows.append(b_row)
                    gam_rows.append(e_gc_t[lb:lb + 1, :])
                    e_tots.append(e_tot[:, lg:lg + 1])
        every = range(DN_GROUP * n_rec)
        low = [lowers[i % n_rec] for i in every]
        gkk = [_dot_nt(ks[i], ks[i]) for i in every]
        gqk = [_dot_nt(qs[i], ks[i]) for i in every]
        a = [jnp.where((rows > cols) if low[i] else (rows < cols), gkk[i] * gams[i], 0.0) * b_rows[i] for i in every]
        qk = [gqk[i] * gams[i] * b_rows[i] for i in every]
        t = _unit_triangular_inverses(a, rows, cols, low)
        us = [_dot(t[i], vs[i]) for i in every]
        ws = [_dot(t[i] * gam_rows[i], ks[i]) for i in every]
        states = [state_ref[r] for r in range(n_rec)]
        for g in range(DN_GROUP):
            rec = range(g * n_rec, (g + 1) * n_rec)
            qws = [_dot(jnp.concatenate([qds[i], ws[i]], axis=0), states[i % n_rec]) for i in rec]
            v_new = [us[i] - x[c_:, :] for i, x in zip(rec, qws)]
            outs = [x[:c_, :] + _dot(qk[i], vn) for i, x, vn in zip(rec, qws, v_new)]
            upd = [_dot(kdts[i], vn) for i, vn in zip(rec, v_new)]
            states = [states[i % n_rec] * e_tots[i] + u for i, u in zip(rec, upd)]
            o_f_ref[pl.ds(r0s[2 * g], c_), :] = jnp.concatenate(outs[:DN_HEADS], axis=1).astype(BF16)
            o_b_ref[pl.ds(r0s[2 * g + 1], c_), :] = jnp.concatenate(outs[DN_HEADS:], axis=1).astype(BF16)
        for r in range(n_rec):
            state_ref[r] = states[r]
        return carry

    lax.fori_loop(0, nc // DN_GROUP, group_step, 0)


def _dn_main(qkvn, gb, bsz, seq):
    blk = SEQ_TILE
    nb = seq // blk
    fwd = lambda b, i: (b, i, 0)
    bwd = lambda b, i: (b, nb - 1 - i, 0)
    return pl.pallas_call(
        _dn_main_kernel,
        grid=(bsz, nb),
        in_specs=[pl.BlockSpec((None, blk, 3 * GROUP_WIDTH), fwd),
                  pl.BlockSpec((None, blk, LANES), fwd),
                  pl.BlockSpec((None, blk, 3 * GROUP_WIDTH), bwd),
                  pl.BlockSpec((None, blk, LANES), bwd)],
        out_specs=[pl.BlockSpec((None, blk, GROUP_WIDTH), fwd),
                   pl.BlockSpec((None, blk, GROUP_WIDTH), bwd)],
        out_shape=[jax.ShapeDtypeStruct((bsz, seq, GROUP_WIDTH), BF16)] * 2,
        scratch_shapes=[pltpu.VMEM((2 * DN_HEADS, DN_DK, DN_DK), F32)],
        compiler_params=_cparams(("parallel", "arbitrary")),
        name="dn_main",
    )(qkvn, gb, qkvn, gb)


def _sgu_kernel(x_ref, g_ref, b_ref, w_ref, bias_ref, o_ref):
    blk = x_ref.shape[0]
    x = x_ref[...].astype(F32)
    x = 0.5 * x * (1.0 + jnp.tanh(np.sqrt(2.0 / np.pi) * (x + 0.044715 * x * x * x)))
    u = x[:, :SGU_WIDTH]
    v = _layer_norm(x[:, SGU_WIDTH:], g_ref[...], b_ref[...]).astype(BF16)
    group = lax.broadcasted_iota(jnp.int32, (SGU_CHUNK, SGU_WIDTH), 1) // (SGU_WIDTH // SGU_GROUPS)
    for c in range(blk // SGU_CHUNK):
        r = slice(c * SGU_CHUNK, (c + 1) * SGU_CHUNK)
        vc = v[r, :]
        mixed = bias_ref[...]
        for gi in range(SGU_GROUPS):
            mixed = mixed + jnp.where(group == gi, _dot(w_ref[gi], vc), 0.0)
        o_ref[r, :] = (u[r, :] * mixed).astype(BF16)


def _sgu(x, ln_g, ln_b, w_s, b_s, bsz, seq):
    t = x.shape[0]
    blk = SEQ_TILE
    bias = jnp.repeat(b_s.T.astype(F32), SGU_WIDTH // SGU_GROUPS, axis=1)
    fixed2 = lambda i: (0, 0)
    return pl.pallas_call(
        _sgu_kernel,
        grid=(t // blk,),
        in_specs=[pl.BlockSpec((blk, 2 * SGU_WIDTH), lambda i: (i, 0)),
                  pl.BlockSpec((1, SGU_WIDTH), fixed2),
                  pl.BlockSpec((1, SGU_WIDTH), fixed2),
                  pl.BlockSpec((SGU_GROUPS, SGU_CHUNK, SGU_CHUNK), lambda i: (0, 0, 0)),
                  pl.BlockSpec((SGU_CHUNK, SGU_WIDTH), fixed2)],
        out_specs=pl.BlockSpec((blk, SGU_WIDTH), lambda i: (i, 0)),
        out_shape=jax.ShapeDtypeStruct((t, SGU_WIDTH), BF16),
        compiler_params=_cparams(("parallel",)),
        name="sgu",
    )(x, ln_g.reshape(1, -1).astype(F32), ln_b.reshape(1, -1).astype(F32), w_s.astype(BF16), bias)


def _pool_kernel(seq, prev_ref, main_ref, next_ref, w_ref, scale_ref, o_ref):
    blk = main_ref.shape[0]
    n = blk + 2 * HALO
    xh = _with_halo(prev_ref, main_ref, next_ref)
    shift = lambda a, k: pltpu.roll(a, k % n, 0)
    s2 = shift(xh, 1) + xh
    s4 = shift(s2, 1) + shift(s2, -1)
    s8 = shift(s4, 2) + shift(s4, -2)
    s16 = shift(s8, 4) + shift(s8, -4)
    sums = (s2, s4, s8, s16)
    x = xh[HALO:HALO + blk, :]
    group = lax.broadcasted_iota(jnp.int32, (blk, GROUP_WIDTH), 1) // POOL_GW
    pos = lax.broadcasted_iota(jnp.int32, (blk, GROUP_WIDTH), 0) + pl.program_id(1) * blk
    wsum = jnp.zeros((blk, GROUP_WIDTH), F32)
    cnt = jnp.ones((blk, GROUP_WIDTH), F32)
    for gi, win in enumerate(POOL_WINDOWS):
        hi = jnp.minimum(pos + win // 2, seq)
        lo = jnp.maximum(pos - win // 2, 0)
        wsum = jnp.where(group == gi, sums[gi][HALO:HALO + blk, :], wsum)
        cnt = jnp.where(group == gi, (hi - lo).astype(F32), cnt)
    pooled = wsum / cnt - x
    o_ref[...] = (_dot(pooled, w_ref[...]) * scale_ref[...]).astype(BF16)


def _pool(x, w_pool, scale, bsz, seq):
    blk = SEQ_TILE
    nb = seq // blk
    x3 = x.reshape(bsz, seq, GROUP_WIDTH)
    prev, main, nxt = _halo_specs(blk, GROUP_WIDTH, nb)
    wbd = jax.scipy.linalg.block_diag(*[w_pool[g].astype(F32) for g in range(len(POOL_WINDOWS))])
    fixed = lambda b, i: (0, 0)
    out = pl.pallas_call(
        functools.partial(_pool_kernel, seq),
        grid=(bsz, nb),
        in_specs=[prev, main, nxt,
                  pl.BlockSpec((GROUP_WIDTH, GROUP_WIDTH), fixed),
                  pl.BlockSpec((1, GROUP_WIDTH), fixed)],
        out_specs=pl.BlockSpec((None, blk, GROUP_WIDTH), lambda b, i: (b, i, 0)),
        out_shape=jax.ShapeDtypeStruct((bsz, seq, GROUP_WIDTH), BF16),
        compiler_params=_cparams(("parallel", "parallel")),
        name="pool",
    )(x3, x3, x3, wbd, scale.reshape(1, -1).astype(F32))
    return out.reshape(bsz * seq, GROUP_WIDTH)


def _rms(x, g):
    return x * lax.rsqrt(jnp.mean(x * x, axis=-1, keepdims=True) + RMS_EPS) * g


def _mla_prep_kernel(x_ref, qg_ref, kvg_ref, wq_ref, wk_ref, wvt_ref, cos_ref, sin_ref, q_ref, k_ref, vt_ref):
    hw = MLA_HEADS * LANES
    cq = _rms(x_ref[:, 0:MLA_Q_RANK].astype(F32), qg_ref[...]).astype(BF16)
    ckv = _rms(x_ref[:, MLA_Q_RANK:MLA_Q_RANK + MLA_KV_RANK].astype(F32), kvg_ref[...]).astype(BF16)
    c0 = MLA_Q_RANK + MLA_KV_RANK
    cos = cos_ref[...]
    sin = sin_ref[...]
    k_rope = (x_ref[:, c0:c0 + LANES].astype(F32) * cos
              + x_ref[:, c0 + LANES:c0 + 2 * LANES].astype(F32) * sin)
    qa = _dot(cq, wq_ref[:, 0:hw])
    qb = _dot(cq, wq_ref[:, hw:2 * hw])
    kn = _dot(ckv, wk_ref[...])
    vt = _dot_nt(wvt_ref[...], ckv).astype(BF16)
    ones = jnp.ones((MLA_VROWS - MLA_V, vt.shape[1]), BF16)
    vt = jnp.concatenate([piece for h in range(MLA_HEADS)
                          for piece in (vt[h * MLA_V:(h + 1) * MLA_V, :], ones)], axis=0)
    for c in range(vt_ref.shape[0]):
        vt_ref[c] = vt[:, c * ATT_TK:(c + 1) * ATT_TK]
    for h in range(MLA_HEADS):
        c = slice(h * LANES, (h + 1) * LANES)
        q_ref[:, c] = (qa[:, c] * cos + qb[:, c] * sin).astype(BF16)
        k_ref[:, c] = (kn[:, c] + k_rope).astype(BF16)


def _rope_tables(seq):
    inv = ROPE_THETA ** (-jnp.arange(0, MLA_ROPE, 2, dtype=F32) / MLA_ROPE)
    ang = jnp.arange(seq, dtype=F32)[:, None] * inv[None, :]
    pad = LANES - MLA_NOPE - MLA_ROPE
    cos = jnp.concatenate([jnp.ones((seq, MLA_NOPE), F32), jnp.cos(ang), jnp.cos(ang), jnp.zeros((seq, pad), F32)], 1)
    sin = jnp.concatenate([jnp.zeros((seq, MLA_NOPE), F32), jnp.sin(ang), jnp.sin(ang), jnp.zeros((seq, pad), F32)], 1)
    return cos, sin


def _pack_mla_weights(w_uq, w_uk, w_uv):
    half = MLA_ROPE // 2
    scale = (MLA_NOPE + MLA_ROPE) ** -0.5 * np.log2(np.e)
    pad = LANES - MLA_NOPE - MLA_ROPE
    qa, qb, kn = [], [], []
    for h in range(MLA_HEADS):
        wq = w_uq[:, h, :].astype(F32) * scale
        rope = wq[:, MLA_NOPE:]
        rot = jnp.concatenate([-rope[:, half:], rope[:, :half]], axis=1)
        zq = lambda n: jnp.zeros((MLA_Q_RANK, n), F32)
        qa.append(jnp.concatenate([wq, zq(pad)], axis=1))
        qb.append(jnp.concatenate([zq(MLA_NOPE), rot, zq(pad)], axis=1))
        kn.append(jnp.concatenate([w_uk[:, h, :].astype(F32), jnp.zeros((MLA_KV_RANK, LANES - MLA_NOPE), F32)], axis=1))
    wq_packed = jnp.concatenate(qa + qb, axis=1).astype(BF16)
    wk_packed = jnp.concatenate(kn, axis=1).astype(BF16)
    wv_t = w_uv.reshape(MLA_KV_RANK, MLA_HEADS * MLA_V).T.astype(BF16)
    return wq_packed, wk_packed, wv_t


def _mla_prep(x, q_g, kv_g, wq_packed, wk_packed, wv_t, cos, sin, bsz, seq):
    t = x.shape[0]
    tm = ROW_TILE
    per_seq = seq // tm
    per_tile = tm // ATT_TK
    hw = MLA_HEADS * LANES
    row = lambda i: (i, 0)
    fixed = lambda i: (0, 0)
    table = lambda i: (i % per_seq, 0)
    return pl.pallas_call(
        _mla_prep_kernel,
        grid=(t // tm,),
        in_specs=[pl.BlockSpec((tm, 768), row),
                  pl.BlockSpec((1, MLA_Q_RANK), fixed),
                  pl.BlockSpec((1, MLA_KV_RANK), fixed),
                  pl.BlockSpec((MLA_Q_RANK, 2 * hw), fixed),
                  pl.BlockSpec((MLA_KV_RANK, hw), fixed),
                  pl.BlockSpec((GROUP_WIDTH, MLA_KV_RANK), fixed),
                  pl.BlockSpec((tm, LANES), table),
                  pl.BlockSpec((tm, LANES), table)],
        out_specs=[pl.BlockSpec((tm, hw), row),
                   pl.BlockSpec((tm, hw), row),
                   pl.BlockSpec((None, per_tile, MLA_HEADS * MLA_VROWS, ATT_TK),
                                lambda i: (i // per_seq, i % per_seq, 0, 0))],
        out_shape=[jax.ShapeDtypeStruct((t, hw), BF16),
                   jax.ShapeDtypeStruct((t, hw), BF16),
                   jax.ShapeDtypeStruct((bsz, seq // ATT_TK, MLA_HEADS * MLA_VROWS, ATT_TK), BF16)],
        compiler_params=_cparams(("parallel",)),
        name="mla_prep",
    )(x, q_g.reshape(1, -1).astype(F32), kv_g.reshape(1, -1).astype(F32), wq_packed, wk_packed, wv_t, cos, sin)


def _attn_kernel(q_ref, k_ref, vt_ref, o_ref):
    n_steps = vt_ref.shape[0]
    tq = q_ref.shape[0]
    tk = ATT_TK
    heads = range(q_ref.shape[1] // LANES)
    unroll = min(n_steps, ATT_UNROLL)
    qs = [q_ref[:, h * LANES:(h + 1) * LANES] for h in heads]

    def scores(h, j):
        r0 = pl.multiple_of(j * tk, tk)
        return _dot_nt(k_ref[pl.ds(r0, tk), h * LANES:(h + 1) * LANES], qs[h])

    def update(h, j, s, m, acc):
        m_new = jnp.maximum(m, jnp.max(s, axis=0, keepdims=True))
        p = jnp.exp2(s - m_new).astype(BF16)
        pv = _dot(vt_ref[j, h * MLA_VROWS:(h + 1) * MLA_VROWS, :], p)
        return m_new, acc * jnp.exp2(m - m_new) + pv

    def group(g, carry):
        base = g * unroll
        state = [list(carry[2 * h:2 * h + 2]) for h in heads]
        pending = [[scores(h, base + u) for h in heads] for u in range(min(ATT_LOOKAHEAD, unroll))]
        for u in range(unroll):
            s_cur = pending.pop(0)
            if u + ATT_LOOKAHEAD < unroll:
                pending.append([scores(h, base + u + ATT_LOOKAHEAD) for h in heads])
            for h in heads:
                state[h] = list(update(h, base + u, s_cur[h], *state[h]))
        return tuple(v for st in state for v in st)

    one = (jnp.full((1, tq), -jnp.inf, F32), jnp.zeros((MLA_VROWS, tq), F32))
    final = lax.fori_loop(0, n_steps // unroll, group, one * len(heads))
    accs = [final[2 * h + 1] for h in heads]
    out_t = jnp.concatenate([acc[:MLA_V, :] / acc[MLA_V:MLA_V + 1, :] for acc in accs], axis=0)
    o_ref[...] = out_t.T.astype(BF16)


def _attention(q, k, vt, bsz, seq):
    hw = MLA_HEADS * LANES
    q3, k3 = (a.reshape(bsz, seq, hw) for a in (q, k))
    tq = ATT_TQ
    hps = MLA_HEADS if seq // ATT_TK <= ATT_ALL_HEADS_MAX_STEPS else 2
    out = pl.pallas_call(
        _attn_kernel,
        grid=(bsz, MLA_HEADS // hps, seq // tq),
        in_specs=[pl.BlockSpec((None, tq, hps * LANES), lambda b, p, i: (b, i, p)),
                  pl.BlockSpec((None, seq, hps * LANES), lambda b, p, i: (b, 0, p)),
                  pl.BlockSpec((None, seq // ATT_TK, hps * MLA_VROWS, ATT_TK), lambda b, p, i: (b, 0, p, 0))],
        out_specs=pl.BlockSpec((None, tq, hps * MLA_V), lambda b, p, i: (b, i, p)),
        out_shape=jax.ShapeDtypeStruct((bsz, seq, GROUP_WIDTH), BF16),
        compiler_params=_cparams(("parallel", "parallel", "arbitrary"), 48),
        name="mla_attention",
    )(q3, k3, vt)
    return out.reshape(bsz * seq, GROUP_WIDTH)


def _out_proj_kernel(pre_ln, x_ref, lng_ref, lnb_ref, of_ref, ob_ref, z_ref, ng_ref, headmean_ref,
                     yb_ref, yc_ref, yd_ref, w_ref, g_ref, b_ref, o_ref):
    x = x_ref[...]
    if pre_ln:
        x = _layer_norm(x, lng_ref[...], lnb_ref[...])
    o = of_ref[...].astype(F32) + ob_ref[...].astype(F32)
    z = z_ref[...].astype(F32)
    ya = o * lax.rsqrt(_dot(o * o, headmean_ref[...]) + RMS_EPS) * ng_ref[...] * _silu(z)
    gw = GROUP_WIDTH
    mix = (_dot(ya.astype(BF16), w_ref[0:gw, :]) + _dot(yb_ref[...], w_ref[gw:2 * gw, :])
           + _dot(yc_ref[...], w_ref[2 * gw:3 * gw, :]) + _dot(yd_ref[...], w_ref[3 * gw:, :]))
    o_ref[...] = _layer_norm(DEEPNORM_ALPHA * x + mix, g_ref[...], b_ref[...])


def _out_proj(x, ln_in_g, ln_in_b, o_f, o_b, z, norm_g, y_b, y_c, y_d, w_out, ln_g, ln_b, pre_ln):
    t = x.shape[0]
    tm = ROW_TILE
    head = np.arange(GROUP_WIDTH) // DN_DK
    headmean = jnp.asarray((head[:, None] == head[None, :]).astype(np.float32) / DN_DK)
    ng = jnp.tile(norm_g.astype(F32), DN_HEADS).reshape(1, GROUP_WIDTH)
    row = lambda i: (i, 0)
    fixed = lambda i: (0, 0)
    quarter = pl.BlockSpec((tm, GROUP_WIDTH), row)
    vec = pl.BlockSpec((1, D_MODEL), fixed)
    return pl.pallas_call(
        functools.partial(_out_proj_kernel, pre_ln),
        grid=(t // tm,),
        in_specs=[pl.BlockSpec((tm, D_MODEL), row), vec, vec,
                  quarter, quarter, quarter,
                  pl.BlockSpec((1, GROUP_WIDTH), fixed),
                  pl.BlockSpec((GROUP_WIDTH, GROUP_WIDTH), fixed),
                  quarter, quarter, quarter,
                  pl.BlockSpec((D_MODEL, D_MODEL), fixed), vec, vec],
        out_specs=pl.BlockSpec((tm, D_MODEL), row),
        out_shape=jax.ShapeDtypeStruct((t, D_MODEL), F32),
        compiler_params=_cparams(("parallel",)),
        name="out_proj",
    )(x, ln_in_g, ln_in_b, o_f, o_b, z, ng, headmean, y_b, y_c, y_d, w_out.astype(BF16),
      ln_g.reshape(1, -1).astype(F32), ln_b.reshape(1, -1).astype(F32))


def _router_kernel(x_ref, whi_ref, wlo_ref, bias_ref, cls_ref, rank_ref, cnt_ref, carry_ref):
    tm = x_ref.shape[0]

    @pl.when(pl.program_id(0) == 0)
    def _():
        carry_ref[...] = jnp.zeros_like(carry_ref)

    x = x_ref[...]
    x_hi = x.astype(BF16)
    x_lo = (x - x_hi.astype(F32)).astype(BF16)
    logits = _dot_nt(whi_ref[...], x_hi) + (_dot_nt(whi_ref[...], x_lo) + _dot_nt(wlo_ref[...], x_hi))
    biased = _sigmoid(logits) + bias_ref[...]
    v = [biased[e:e + 1, :] for e in range(N_EXPERTS)]
    best = chosen = pair = None
    for g in range(N_EXPERT_GROUPS):
        m = v[g * EXPERTS_PER_GROUP:(g + 1) * EXPERTS_PER_GROUP]
        score = functools.reduce(jnp.maximum, [m[i] + m[j] for i, j in PAIRS])
        picked = []
        for e in range(EXPERTS_PER_GROUP):
            outranked = [(m[o] >= m[e]) if o < e else (m[o] > m[e]) for o in range(EXPERTS_PER_GROUP) if o != e]
            picked.append(functools.reduce(jnp.add, [c.astype(jnp.int32) for c in outranked]) < 2)
        first = jnp.where(picked[0], 0, jnp.where(picked[1], 1, 2))
        last = jnp.where(picked[3], 3, jnp.where(picked[2], 2, 1))
        pair_g = (first * (7 - first)) // 2 + (last - first - 1)
        if g == 0:
            best, chosen, pair = score, jnp.zeros_like(pair_g), pair_g
        else:
            better = score > best
            best = jnp.where(better, score, best)
            chosen = jnp.where(better, g, chosen)
            pair = jnp.where(better, pair_g, pair)
    cls_row = (chosen * len(PAIRS) + pair).astype(F32)
    onehot = (lax.broadcasted_iota(jnp.int32, (CLASS_ROWS, tm), 0).astype(F32) == cls_row)
    upto = (lax.broadcasted_iota(jnp.int32, (tm, tm), 0) <= lax.broadcasted_iota(jnp.int32, (tm, tm), 1))
    prefix = _dot(onehot.astype(BF16), upto.astype(BF16))
    carry = carry_ref[:, 0:1]
    rank = jnp.sum(jnp.where(onehot, prefix + carry, 0.0), axis=0, keepdims=True) - 1.0
    cls_ref[...] = jnp.broadcast_to(cls_row, (HALO, tm))
    rank_ref[...] = jnp.broadcast_to(rank, (HALO, tm))
    total = carry + prefix[:, tm - 1:tm]
    carry_ref[...] = jnp.broadcast_to(total, carry_ref.shape)
    cnt_ref[...] = jnp.broadcast_to(total, cnt_ref.shape)


def _router(x, router_w, router_bias):
    t = x.shape[0]
    tm = ROW_TILE
    w = router_w.astype(F32).T
    w_hi = w.astype(BF16)
    w_lo = (w - w_hi.astype(F32)).astype(BF16)
    bias = router_bias.astype(F32).reshape(N_EXPERTS, 1)
    fixed = lambda i: (0, 0)
    cls, rank, cnt = pl.pallas_call(
        _router_kernel,
        grid=(t // tm,),
        in_specs=[pl.BlockSpec((tm, D_MODEL), lambda i: (i, 0)),
                  pl.BlockSpec((N_EXPERTS, D_MODEL), fixed),
                  pl.BlockSpec((N_EXPERTS, D_MODEL), fixed),
                  pl.BlockSpec((N_EXPERTS, 1), fixed)],
        out_specs=[pl.BlockSpec((HALO, tm), lambda i: (0, i)),
                   pl.BlockSpec((HALO, tm), lambda i: (0, i)),
                   pl.BlockSpec((CLASS_ROWS, LANES), fixed)],
        out_shape=[jax.ShapeDtypeStruct((HALO, t), F32),
                   jax.ShapeDtypeStruct((HALO, t), F32),
                   jax.ShapeDtypeStruct((CLASS_ROWS, LANES), F32)],
        scratch_shapes=[pltpu.VMEM((CLASS_ROWS, LANES), F32)],
        compiler_params=_cparams(("arbitrary",)),
        name="router",
    )(x, w_hi, w_lo, bias)
    return cls[0].astype(jnp.int32), rank[0].astype(jnp.int32), cnt[:N_CLASSES, 0].astype(jnp.int32)


def _routing_tables(cls, rank, counts, n_tiles):
    tm = MOE_TILE
    tiles = (counts + tm - 1) // tm
    first_tile = jnp.cumsum(tiles) - tiles
    dest = first_tile[cls] * tm + rank
    tile_ids = jnp.arange(n_tiles, dtype=jnp.int32)
    tile_cls = jnp.sum((jnp.cumsum(tiles)[None, :] <= tile_ids[:, None]).astype(jnp.int32), axis=1)
    tile_cls = jnp.minimum(tile_cls, N_CLASSES - 1)
    used = (tile_ids < jnp.sum(tiles)).astype(jnp.int32)
    pair = np.asarray(PAIRS, np.int32)
    base = (tile_cls // len(PAIRS)) * EXPERTS_PER_GROUP
    e1 = base + jnp.asarray(pair[:, 0])[tile_cls % len(PAIRS)]
    e2 = base + jnp.asarray(pair[:, 1])[tile_cls % len(PAIRS)]
    return dest.astype(jnp.int32), e1.astype(jnp.int32), e2.astype(jnp.int32), used


def _dispatch_kernel(dest_ref, x_ref, zeros_ref, xs_ref, sem):
    del zeros_ref
    tr = x_ref.shape[0]

    def row_copy(r):
        return pltpu.make_async_copy(x_ref.at[pl.ds(r, 1), :], xs_ref.at[pl.ds(dest_ref[0, 0, r], 1), :], sem)

    for r in range(tr):
        row_copy(r).start()
    for r in range(tr):
        row_copy(r).wait()


def _dispatch(x, dest, slots):
    t = x.shape[0]
    tr = DMA_TILE
    dest3 = dest.reshape(t // tr, 1, tr)
    return pl.pallas_call(
        _dispatch_kernel,
        grid=(t // tr,),
        in_specs=[pl.BlockSpec((1, 1, tr), lambda i: (i, 0, 0), memory_space=pltpu.SMEM),
                  pl.BlockSpec((tr, D_MODEL), lambda i: (i, 0)),
                  pl.BlockSpec(memory_space=pl.ANY)],
        out_specs=pl.BlockSpec(memory_space=pl.ANY),
        out_shape=jax.ShapeDtypeStruct(slots.shape, F32),
        scratch_shapes=[pltpu.SemaphoreType.DMA],
        input_output_aliases={2: 0},
        compiler_params=_cparams(("arbitrary",)),
        name="moe_dispatch",
    )(dest3, x, slots)


def _moe_kernel(e1_ref, e2_ref, used_ref, x_ref, rw_ref, wg1_ref, wu1_ref, wd1_ref, wg2_ref, wu2_ref, wd2_ref, y_ref):
    i = pl.program_id(0)

    @pl.when(used_ref[i] == 1)
    def _():
        xb = x_ref[...].astype(BF16)
        scores = _sigmoid(_dot(xb, rw_ref[...]))
        lane = lax.broadcasted_iota(jnp.int32, scores.shape, 1)
        s1 = jnp.sum(jnp.where(lane == e1_ref[i], scores, 0.0), axis=-1, keepdims=True)
        s2 = jnp.sum(jnp.where(lane == e2_ref[i], scores, 0.0), axis=-1, keepdims=True)
        inv = 1.0 / (s1 + s2)
        h1 = _silu(_dot(xb, wg1_ref[...])) * _dot(xb, wu1_ref[...]) * (s1 * inv)
        h2 = _silu(_dot(xb, wg2_ref[...])) * _dot(xb, wu2_ref[...]) * (s2 * inv)
        y_ref[...] = _dot(h1.astype(BF16), wd1_ref[...]) + _dot(h2.astype(BF16), wd2_ref[...])

    @pl.when(used_ref[i] == 0)
    def _():
        y_ref[...] = jnp.zeros_like(y_ref)


def _moe(xs, e1, e2, used, router_w, w_gate, w_up, w_down):
    n_slots = xs.shape[0]
    tm = MOE_TILE
    rw = jnp.zeros((D_MODEL, LANES), BF16).at[:, :N_EXPERTS].set(router_w.astype(BF16))
    up = lambda sel: pl.BlockSpec((None, D_MODEL, EXPERT_FF), lambda i, e1, e2, u: ((e1, e2)[sel][i], 0, 0))
    down = lambda sel: pl.BlockSpec((None, EXPERT_FF, D_MODEL), lambda i, e1, e2, u: ((e1, e2)[sel][i], 0, 0))
    grid_spec = pltpu.PrefetchScalarGridSpec(
        num_scalar_prefetch=3,
        grid=(n_slots // tm,),
        in_specs=[pl.BlockSpec((tm, D_MODEL), lambda i, e1, e2, u: (i, 0)),
                  pl.BlockSpec((D_MODEL, LANES), lambda i, e1, e2, u: (0, 0)),
                  up(0), up(0), down(0), up(1), up(1), down(1)],
        out_specs=pl.BlockSpec((tm, D_MODEL), lambda i, e1, e2, u: (i, 0)),
    )
    return pl.pallas_call(
        _moe_kernel,
        grid_spec=grid_spec,
        out_shape=jax.ShapeDtypeStruct((n_slots, D_MODEL), F32),
        compiler_params=_cparams(("arbitrary",), 48),
        name="moe_experts",
    )(e1, e2, used, xs, rw, w_gate, w_up, w_down, w_gate, w_up, w_down)


def _combine_kernel(dest_ref, dest_next_ref, x_ref, ys_ref, g_ref, b_ref, o_ref, buf, sem):
    tr = x_ref.shape[0]
    i = pl.program_id(0)
    slot = i % 2

    def row_copy(idx_ref, r, s):
        return pltpu.make_async_copy(ys_ref.at[pl.ds(idx_ref[0, 0, r], 1), :], buf.at[s, pl.ds(r, 1), :], sem.at[s])

    @pl.when(i == 0)
    def _():
        for r in range(tr):
            row_copy(dest_ref, r, slot).start()

    @pl.when(i + 1 < pl.num_programs(0))
    def _():
        for r in range(tr):
            row_copy(dest_next_ref, r, 1 - slot).start()

    for r in range(tr):
        row_copy(dest_ref, r, slot).wait()
    o_ref[...] = _layer_norm(DEEPNORM_ALPHA * x_ref[...] + buf[slot], g_ref[...], b_ref[...])


def _combine(x, ys, dest, ln_g, ln_b):
    t = x.shape[0]
    tr = DMA_TILE
    n = t // tr
    dest3 = dest.reshape(n, 1, tr)
    fixed = lambda i: (0, 0)
    return pl.pallas_call(
        _combine_kernel,
        grid=(n,),
        in_specs=[pl.BlockSpec((1, 1, tr), lambda i: (i, 0, 0), memory_space=pltpu.SMEM),
                  pl.BlockSpec((1, 1, tr), lambda i: (jnp.minimum(i + 1, n - 1), 0, 0), memory_space=pltpu.SMEM),
                  pl.BlockSpec((tr, D_MODEL), lambda i: (i, 0)),
                  pl.BlockSpec(memory_space=pl.ANY),
                  pl.BlockSpec((1, D_MODEL), fixed),
                  pl.BlockSpec((1, D_MODEL), fixed)],
        out_specs=pl.BlockSpec((tr, D_MODEL), lambda i: (i, 0)),
        out_shape=jax.ShapeDtypeStruct((t, D_MODEL), F32),
        scratch_shapes=[pltpu.VMEM((2, tr, D_MODEL), F32), pltpu.SemaphoreType.DMA((2,))],
        compiler_params=_cparams(("arbitrary",)),
        name="moe_combine",
    )(dest3, dest3, x, ys, ln_g.reshape(1, -1).astype(F32), ln_b.reshape(1, -1).astype(F32))


def _trunk(x, p, slots):
    bsz, seq, d = x.shape
    t = bsz * seq
    x = x.reshape(t, d)
    cos, sin = _rope_tables(seq)
    n_tiles = t // MOE_TILE + N_CLASSES
    for l in range(DEPTH):
        first = l == 0
        qkv, z, sgu_in, pool_in, mla_in, misc = _in_proj(x, p["ln_in_g"], p["ln_in_b"], p["w_in"][l], first)
        qkvn, gb = _dn_prep(qkv, misc, p["dn_conv_w"][l], p["dn_a_log"][l], p["dn_dt_bias"][l], bsz, seq)
        o_f, o_b = _dn_main(qkvn, gb, bsz, seq)
        y_b = _sgu(sgu_in, p["sgu_ln_g"][l], p["sgu_ln_b"][l], p["sgu_w"][l], p["sgu_b"][l], bsz, seq)
        y_c = _pool(pool_in, p["pool_w"][l], p["pool_scale"][l], bsz, seq)
        q, k, vt = _mla_prep(mla_in, p["mla_q_norm_g"][l], p["mla_kv_norm_g"][l], p["wq"][l], p["wk"][l],
                             p["wvt"][l], cos, sin, bsz, seq)
        y_d = _attention(q, k, vt, bsz, seq)
        x = _out_proj(x, p["ln_in_g"], p["ln_in_b"], o_f.reshape(t, -1), o_b.reshape(t, -1), z,
                      p["dn_norm_g"][l], y_b, y_c, y_d, p["w_out"][l], p["ln1_g"][l], p["ln1_b"][l], first)
        cls, rank, counts = _router(x, p["router_w"], p["router_bias"])
        dest, e1, e2, used = _routing_tables(cls, rank, counts, n_tiles)
        if slots is None or slots.shape[0] != n_tiles * MOE_TILE:
            slots = jnp.zeros((n_tiles * MOE_TILE, D_MODEL), F32)
        slots = _dispatch(x, dest, slots)
        ys = _moe(slots, e1, e2, used, p["router_w"], p["moe_w_gate"][l], p["moe_w_up"][l], p["moe_w_down"][l])
        x = _combine(x, ys, dest, p["ln2_g"][l], p["ln2_b"][l])
    return x.reshape(bsz, seq, d), slots


def kernel(x_prompt, x_sample, ln_in_g, ln_in_b, w_in, dn_conv_w, dn_a_log, dn_dt_bias, dn_norm_g, sgu_ln_g, sgu_ln_b, sgu_w, sgu_b, pool_w, pool_scale, mla_q_norm_g, mla_kv_norm_g, mla_w_uq, mla_w_uk, mla_w_uv, w_out, ln1_g, ln1_b, router_w, router_bias, moe_w_gate, moe_w_up, moe_w_down, ln2_g, ln2_b):
    packed_mla = [_pack_mla_weights(mla_w_uq[l], mla_w_uk[l], mla_w_uv[l]) for l in range(DEPTH)]
    p = dict(
        ln_in_g=ln_in_g.reshape(1, -1).astype(F32), ln_in_b=ln_in_b.reshape(1, -1).astype(F32),
        w_in=[_pack_in_weights(w_in[l]) for l in range(DEPTH)],
        dn_conv_w=dn_conv_w, dn_a_log=dn_a_log, dn_dt_bias=dn_dt_bias, dn_norm_g=dn_norm_g,
        sgu_ln_g=sgu_ln_g, sgu_ln_b=sgu_ln_b, sgu_w=sgu_w, sgu_b=sgu_b,
        pool_w=pool_w, pool_scale=pool_scale,
        mla_q_norm_g=mla_q_norm_g, mla_kv_norm_g=mla_kv_norm_g,
        wq=[w[0] for w in packed_mla], wk=[w[1] for w in packed_mla], wvt=[w[2] for w in packed_mla],
        w_out=w_out, ln1_g=ln1_g, ln1_b=ln1_b, router_w=router_w, router_bias=router_bias,
        moe_w_gate=moe_w_gate.astype(BF16), moe_w_up=moe_w_up.astype(BF16), moe_w_down=moe_w_down.astype(BF16),
        ln2_g=ln2_g, ln2_b=ln2_b,
    )
    y_prompt, slots = _trunk(x_prompt, p, None)
    y_sample, _ = _trunk(x_sample, p, slots)
    return y_prompt, y_sample
```

```python
import functools

import numpy as np
import jax
import jax.numpy as jnp
from jax import lax
from jax.experimental import pallas as pl
from jax.experimental.pallas import tpu as pltpu

F32 = jnp.float32
BF16 = jnp.bfloat16

D_MODEL = 1024
DEPTH = 2
GROUP_WIDTH = D_MODEL // 4
DN_HEADS = 4
DN_DK = GROUP_WIDTH // DN_HEADS
DN_CONV = 5
DN_CHUNK = 64
DN_GROUP = 4
SGU_GROUPS = 4
SGU_CHUNK = 128
SGU_WIDTH = GROUP_WIDTH
POOL_WINDOWS = (2, 4, 8, 16)
POOL_GW = GROUP_WIDTH // len(POOL_WINDOWS)
MLA_HEADS = 4
MLA_NOPE = 64
MLA_ROPE = 32
MLA_V = GROUP_WIDTH // MLA_HEADS
MLA_VROWS = MLA_V + 16
MLA_Q_RANK = 384
MLA_KV_RANK = 128
ROPE_THETA = 10000.0
N_EXPERTS = 16
N_EXPERT_GROUPS = 4
EXPERTS_PER_GROUP = N_EXPERTS // N_EXPERT_GROUPS
EXPERT_FF = 512
DEEPNORM_ALPHA = (2 * DEPTH) ** 0.25
LN_EPS = 1e-5
RMS_EPS = 1e-6

LANES = 128
HALO = 8

PAIRS = ((0, 1), (0, 2), (0, 3), (1, 2), (1, 3), (2, 3))
N_CLASSES = N_EXPERT_GROUPS * len(PAIRS)
CLASS_ROWS = 32

C_QKV = 0
C_Z = 768
C_SGU = 1024
C_POOL = 1536
C_MLA = 1792
C_MISC = 2560
IN_PACKED = 2688

ROW_TILE = 512
SEQ_TILE = 512
DN_BLOCK = 1024
MOE_TILE = 256
DMA_TILE = 512
ATT_TQ = 256
ATT_TK = 256
ATT_UNROLL = 32
ATT_ALL_HEADS_MAX_STEPS = 8
ATT_LOOKAHEAD = 3


def _cparams(sem, vmem_mb=None):
    kw = dict(dimension_semantics=sem)
    if vmem_mb is not None:
        kw["vmem_limit_bytes"] = vmem_mb * 1024 * 1024
    return pltpu.CompilerParams(**kw)


def _layer_norm(x, g, b):
    mu = jnp.mean(x, axis=-1, keepdims=True)
    xc = x - mu
    var = jnp.mean(xc * xc, axis=-1, keepdims=True)
    return xc * lax.rsqrt(var + LN_EPS) * g + b


def _sigmoid(x):
    return 1.0 / (1.0 + jnp.exp(-x))


def _silu(x):
    return x * _sigmoid(x)


def _dot(a, b):
    return jnp.dot(a, b, preferred_element_type=F32)


def _dot_nt(a, b):
    return lax.dot_general(a, b, (((1,), (1,)), ((), ())), preferred_element_type=F32)


def _dot_tn(a, b):
    return lax.dot_general(a, b, (((0,), (0,)), ((), ())), preferred_element_type=F32)


def _in_proj_kernel(pre_ln, x_ref, g_ref, b_ref, w_ref,
                    qkv_ref, z_ref, sgu_ref, pool_ref, mla_ref, misc_ref):
    x = x_ref[...]
    if pre_ln:
        x = _layer_norm(x, g_ref[...], b_ref[...])
    xb = x.astype(BF16)
    qkv_ref[...] = _dot(xb, w_ref[:, C_QKV:C_Z]).astype(BF16)
    z_ref[...] = _dot(xb, w_ref[:, C_Z:C_SGU]).astype(BF16)
    sgu_ref[...] = _dot(xb, w_ref[:, C_SGU:C_POOL]).astype(BF16)
    pool_ref[...] = _dot(xb, w_ref[:, C_POOL:C_MLA]).astype(BF16)
    mla_ref[...] = _dot(xb, w_ref[:, C_MLA:C_MISC]).astype(BF16)
    misc_ref[...] = _dot(xb, w_ref[:, C_MISC:IN_PACKED])


def _in_proj(x, ln_g, ln_b, w_packed, pre_ln):
    t = x.shape[0]
    tm = ROW_TILE
    widths = (768, 256, 512, 256, 768, 128)
    dtypes = (BF16, BF16, BF16, BF16, BF16, F32)
    row = lambda i: (i, 0)
    fixed = lambda i: (0, 0)
    return pl.pallas_call(
        functools.partial(_in_proj_kernel, pre_ln),
        grid=(t // tm,),
        in_specs=[pl.BlockSpec((tm, D_MODEL), row),
                  pl.BlockSpec((1, D_MODEL), fixed),
                  pl.BlockSpec((1, D_MODEL), fixed),
                  pl.BlockSpec((D_MODEL, IN_PACKED), fixed)],
        out_specs=[pl.BlockSpec((tm, w), row) for w in widths],
        out_shape=[jax.ShapeDtypeStruct((t, w), d) for w, d in zip(widths, dtypes)],
        compiler_params=_cparams(("parallel",), 48),
        name="in_proj",
    )(x, ln_g, ln_b, w_packed)


def _pack_in_weights(w):
    d = w.shape[0]
    zeros = lambda n: jnp.zeros((d, n), w.dtype)
    qkv, z = w[:, 0:768], w[:, 768:1024]
    bb, aa = w[:, 1024:1032], w[:, 1032:1040]
    sgu, pool = w[:, 1040:1552], w[:, 1552:1808]
    cq, ckv, kr = w[:, 1808:2192], w[:, 2192:2320], w[:, 2320:2352]
    half = MLA_ROPE // 2
    kr_rot = jnp.concatenate([-kr[:, half:], kr[:, :half]], axis=1)
    rope_tile = jnp.concatenate([zeros(MLA_NOPE), kr, zeros(LANES - MLA_NOPE - MLA_ROPE)], axis=1)
    rot_tile = jnp.concatenate([zeros(MLA_NOPE), kr_rot, zeros(LANES - MLA_NOPE - MLA_ROPE)], axis=1)
    misc = jnp.concatenate([bb, aa, zeros(LANES - 16)], axis=1)
    return jnp.concatenate([qkv, z, sgu, pool, cq, ckv, rope_tile, rot_tile, misc], axis=1).astype(BF16)


def _halo_specs(blk, width, n_seq_blocks):
    per = blk // HALO
    last = n_seq_blocks * per - 1
    main = pl.BlockSpec((None, blk, width), lambda b, i: (b, i, 0))
    prev = pl.BlockSpec((None, HALO, width), lambda b, i: (b, jnp.maximum(i * per - 1, 0), 0))
    nxt = pl.BlockSpec((None, HALO, width), lambda b, i: (b, jnp.minimum((i + 1) * per, last), 0))
    return prev, main, nxt


def _with_halo(prev_ref, main_ref, next_ref):
    i = pl.program_id(1)
    n = pl.num_programs(1)
    prev = jnp.where(i > 0, prev_ref[...].astype(F32), 0.0)
    nxt = jnp.where(i < n - 1, next_ref[...].astype(F32), 0.0)
    return jnp.concatenate([prev, main_ref[...].astype(F32), nxt], axis=0)


def _dn_prep_kernel(prev_ref, main_ref, next_ref, misc_ref, convw_ref, decay_a_ref, dtb_ref, headsum_ref,
                    qkvn_ref, gb_ref):
    blk = main_ref.shape[0]
    xh = _with_halo(prev_ref, main_ref, next_ref)
    acc = jnp.zeros((blk, 3 * GROUP_WIDTH), F32)
    n = blk + 2 * HALO
    for j in range(DN_CONV):
        tap = xh if j == DN_CONV // 2 else pltpu.roll(xh, (DN_CONV // 2 - j) % n, 0)
        acc = acc + tap[HALO:HALO + blk, :] * convw_ref[j:j + 1, :]
    x = _silu(acc)
    q = x[:, 0:GROUP_WIDTH]
    k = x[:, GROUP_WIDTH:2 * GROUP_WIDTH]
    v = x[:, 2 * GROUP_WIDTH:]
    hs = headsum_ref[...]
    q = q * lax.rsqrt(_dot(q * q, hs) + RMS_EPS) * (DN_DK ** -0.5)
    k = k * lax.rsqrt(_dot(k * k, hs) + RMS_EPS)
    qkvn_ref[:, 0:GROUP_WIDTH] = q.astype(BF16)
    qkvn_ref[:, GROUP_WIDTH:2 * GROUP_WIDTH] = k.astype(BF16)
    qkvn_ref[:, 2 * GROUP_WIDTH:] = v.astype(BF16)
    m = misc_ref[...]
    lane = lax.broadcasted_iota(jnp.int32, m.shape, 1)
    xs = m + dtb_ref[...]
    softplus = jnp.maximum(xs, 0.0) + jnp.log1p(jnp.exp(-jnp.abs(xs)))
    gb = jnp.where(lane < 2 * DN_HEADS, _sigmoid(m), -decay_a_ref[...] * softplus)
    gb_ref[...] = jnp.where(lane < 4 * DN_HEADS, gb, 0.0)


def _dn_prep(qkv, misc, conv_w, a_log, dt_bias, bsz, seq):
    blk = SEQ_TILE
    nb = seq // blk
    qkv3 = qkv.reshape(bsz, seq, 3 * GROUP_WIDTH)
    misc3 = misc.reshape(bsz, seq, LANES)
    prev, main, nxt = _halo_specs(blk, 3 * GROUP_WIDTH, nb)
    convw = jnp.zeros((HALO, 3 * GROUP_WIDTH), F32).at[:DN_CONV].set(conv_w.astype(F32))
    lane_vec = lambda v: jnp.zeros((1, LANES), F32).at[0, 2 * DN_HEADS:4 * DN_HEADS].set(v.reshape(-1).astype(F32))
    decay_a = lane_vec(jnp.exp(a_log.astype(F32)))
    dtb = lane_vec(dt_bias)
    head = np.arange(GROUP_WIDTH) // DN_DK
    headsum = jnp.asarray((head[:, None] == head[None, :]).astype(np.float32))
    fixed = lambda b, i: (0, 0)
    qkvn, gb = pl.pallas_call(
        _dn_prep_kernel,
        grid=(bsz, nb),
        in_specs=[prev, main, nxt,
                  pl.BlockSpec((None, blk, LANES), lambda b, i: (b, i, 0)),
                  pl.BlockSpec((HALO, 3 * GROUP_WIDTH), fixed),
                  pl.BlockSpec((1, LANES), fixed),
                  pl.BlockSpec((1, LANES), fixed),
                  pl.BlockSpec((GROUP_WIDTH, GROUP_WIDTH), fixed)],
        out_specs=[pl.BlockSpec((None, blk, 3 * GROUP_WIDTH), lambda b, i: (b, i, 0)),
                   pl.BlockSpec((None, blk, LANES), lambda b, i: (b, i, 0))],
        out_shape=[jax.ShapeDtypeStruct((bsz, seq, 3 * GROUP_WIDTH), BF16),
                   jax.ShapeDtypeStruct((bsz, seq, LANES), F32)],
        compiler_params=_cparams(("parallel", "parallel")),
        name="dn_prep",
    )(qkv3, qkv3, qkv3, misc3, convw, decay_a, dtb, headsum)
    return qkvn, gb


def _unit_triangular_inverses(mats, rows, cols, lowers):
    c = mats[0].shape[0]
    eye = (rows == cols).astype(F32)

    def level_mask(lower, s):
        hi, lo = (rows, cols) if lower else (cols, rows)
        return ((hi // s) == (lo // s) + 1) & ((hi // (2 * s)) == (lo // (2 * s)))

    ts = [eye - jnp.where(level_mask(lower, 1), a, 0.0) for a, lower in zip(mats, lowers)]
    s = 2
    while s < c:
        tc = [_dot(t, jnp.where(level_mask(lower, s), a, 0.0)) for t, a, lower in zip(ts, mats, lowers)]
        yield
        ts = [t - _dot(x, t) for t, x in zip(ts, tc)]
        yield
        s *= 2
    return ts


def _interleave(first, second):
    results = [None, None]
    live = [first, second]
    while any(g is not None for g in live):
        for i, g in enumerate(live):
            if g is not None:
                try:
                    next(g)
                except StopIteration as stop:
                    results[i] = stop.value
                    live[i] = None
    return results


def _dn_main_kernel(qkv_f_ref, gb_f_ref, qkv_b_ref, gb_b_ref, o_f_ref, o_b_ref, state_ref):
    blk = qkv_f_ref.shape[0]
    nc = blk // DN_CHUNK
    c_ = DN_CHUNK

    @pl.when(pl.program_id(1) == 0)
    def _():
        state_ref[...] = jnp.zeros_like(state_ref)

    rows = lax.broadcasted_iota(jnp.int32, (c_, c_), 0)
    cols = lax.broadcasted_iota(jnp.int32, (c_, c_), 1)
    tri_lo = (rows >= cols).astype(F32)
    tri_up = (rows <= cols).astype(F32)
    n_rec = 2 * DN_HEADS
    lowers = [i < DN_HEADS for i in range(n_rec)]

    def transpose_chunk(m):
        return jnp.concatenate([m, jnp.zeros((LANES - c_, LANES), F32)], axis=0).T[:, :c_]

    lane = lax.broadcasted_iota(jnp.int32, (c_, LANES), 1)

    def prepare(gi):
        r0s, qs, qds, ks, vs, kdts, gams, b_rows, gam_rows, e_tots = [], [], [], [], [], [], [], [], [], []
        for g in range(DN_GROUP):
            ci = gi * DN_GROUP + g
            for d in range(2):
                lower = d == 0
                qkv_ref, gb_ref = (qkv_f_ref, gb_f_ref) if lower else (qkv_b_ref, gb_b_ref)
                r0 = (ci if lower else nc - 1 - ci) * c_
                r0s.append(r0)
                x = qkv_ref[r0:r0 + c_, :].astype(F32)
                gbt = gb_ref[r0:r0 + c_, :]
                incl = (rows >= cols) if lower else (rows <= cols)
                gc = jnp.dot(tri_lo if lower else tri_up, gbt, precision=lax.Precision.HIGHEST,
                             preferred_element_type=F32)
                stats_t = transpose_chunk(jnp.where(lane < n_rec, gbt, gc))
                beta_t = stats_t[0:n_rec, :]
                gc_t = stats_t[n_rec:2 * n_rec, :]
                g_last_t = gc_t[:, c_ - 1:c_] if lower else gc_t[:, 0:1]
                e_gc_t = jnp.exp(gc_t)
                e_rest_t = jnp.exp(g_last_t - gc_t)
                e_tot = jnp.exp(gc[c_ - 1:c_, :] if lower else gc[0:1, :])
                k_t = [transpose_chunk(x[:, GROUP_WIDTH + j * LANES:GROUP_WIDTH + (j + 1) * LANES])
                       for j in range(GROUP_WIDTH // LANES)]
                for h in range(DN_HEADS):
                    lb = DN_HEADS * d + h
                    lg = n_rec + lb
                    b_row = beta_t[lb:lb + 1, :]
                    gc_col = jnp.broadcast_to(gc[:, lg:lg + 1], (c_, c_))
                    diff = gc_col - gc_t[lb:lb + 1, :]
                    half = (h * DN_DK) % LANES
                    qs.append(x[:, h * DN_DK:(h + 1) * DN_DK])
                    qds.append(qs[-1] * jnp.exp(gc_col))
                    ks.append(x[:, GROUP_WIDTH + h * DN_DK:GROUP_WIDTH + (h + 1) * DN_DK])
                    vs.append(x[:, 2 * GROUP_WIDTH + h * DN_DK:2 * GROUP_WIDTH + (h + 1) * DN_DK])
                    kdts.append(k_t[(h * DN_DK) // LANES][half:half + DN_DK, :] * (e_rest_t[lb:lb + 1, :] * b_row))
                    gams.append(jnp.where(incl, jnp.exp(jnp.where(incl, diff, 0.0)), 0.0))
                    b_rows.append(b_row)
                    gam_rows.append(e_gc_t[lb:lb + 1, :])
                    e_tots.append(e_tot[:, lg:lg + 1])
        every = range(DN_GROUP * n_rec)
        low = [lowers[i % n_rec] for i in every]
        gkk = [_dot_nt(ks[i], ks[i]) for i in every]
        yield
        gqk = [_dot_nt(qs[i], ks[i]) for i in every]
        yield
        a = [jnp.where((rows > cols) if low[i] else (rows < cols), gkk[i] * gams[i], 0.0) * b_rows[i] for i in every]
        qk = [gqk[i] * gams[i] * b_rows[i] for i in every]
        t = yield from _unit_triangular_inverses(a, rows, cols, low)
        us = [_dot(t[i], vs[i]) for i in every]
        yield
        ws = [_dot(t[i] * gam_rows[i], ks[i]) for i in every]
        yield
        return r0s, qds, ws, us, qk, kdts, e_tots

    def recur(prepared, states):
        r0s, qds, ws, us, qk, kdts, e_tots = prepared
        for g in range(DN_GROUP):
            rec = range(g * n_rec, (g + 1) * n_rec)
            qws = [_dot(jnp.concatenate([qds[i], ws[i]], axis=0), states[i % n_rec]) for i in rec]
            yield
            v_new = [us[i] - x[c_:, :] for i, x in zip(rec, qws)]
            outs = [x[:c_, :] + _dot(qk[i], vn) for i, x, vn in zip(rec, qws, v_new)]
            upd = [_dot(kdts[i], vn) for i, vn in zip(rec, v_new)]
            yield
            states = [states[i % n_rec] * e_tots[i] + u for i, u in zip(rec, upd)]
            r_f, r_b = r0s[2 * g], r0s[2 * g + 1]
            o_f_ref[r_f:r_f + c_, :] = jnp.concatenate(outs[:DN_HEADS], axis=1).astype(BF16)
            o_b_ref[r_b:r_b + c_, :] = jnp.concatenate(outs[DN_HEADS:], axis=1).astype(BF16)
        return states

    n_groups = nc // DN_GROUP
    states = [state_ref[r] for r in range(n_rec)]
    _, prepared = _interleave(iter(()), prepare(0))
    for gi in range(n_groups):
        following = prepare(gi + 1) if gi + 1 < n_groups else iter(())
        states, prepared = _interleave(recur(prepared, states), following)
    for r in range(n_rec):
        state_ref[r] = states[r]


def _dn_main(qkvn, gb, bsz, seq):
    blk = min(DN_BLOCK, seq)
    assert seq % blk == 0 and blk % (DN_GROUP * DN_CHUNK) == 0
    nb = seq // blk
    fwd = lambda b, i: (b, i, 0)
    bwd = lambda b, i: (b, nb - 1 - i, 0)
    return pl.pallas_call(
        _dn_main_kernel,
        grid=(bsz, nb),
        in_specs=[pl.BlockSpec((None, blk, 3 * GROUP_WIDTH), fwd),
                  pl.BlockSpec((None, blk, LANES), fwd),
                  pl.BlockSpec((None, blk, 3 * GROUP_WIDTH), bwd),
                  pl.BlockSpec((None, blk, LANES), bwd)],
        out_specs=[pl.BlockSpec((None, blk, GROUP_WIDTH), fwd),
                   pl.BlockSpec((None, blk, GROUP_WIDTH), bwd)],
        out_shape=[jax.ShapeDtypeStruct((bsz, seq, GROUP_WIDTH), BF16)] * 2,
        scratch_shapes=[pltpu.VMEM((2 * DN_HEADS, DN_DK, DN_DK), F32)],
        compiler_params=_cparams(("parallel", "arbitrary")),
        name="dn_main",
    )(qkvn, gb, qkvn, gb)


def _sgu_kernel(x_ref, g_ref, b_ref, w_ref, bias_ref, o_ref):
    blk = x_ref.shape[0]
    x = x_ref[...].astype(F32)
    x = 0.5 * x * (1.0 + jnp.tanh(np.sqrt(2.0 / np.pi) * (x + 0.044715 * x * x * x)))
    u = x[:, :SGU_WIDTH]
    v = _layer_norm(x[:, SGU_WIDTH:], g_ref[...], b_ref[...]).astype(BF16)
    group = lax.broadcasted_iota(jnp.int32, (SGU_CHUNK, SGU_WIDTH), 1) // (SGU_WIDTH // SGU_GROUPS)
    for c in range(blk // SGU_CHUNK):
        r = slice(c * SGU_CHUNK, (c + 1) * SGU_CHUNK)
        vc = v[r, :]
        mixed = bias_ref[...]
        for gi in range(SGU_GROUPS):
            mixed = mixed + jnp.where(group == gi, _dot(w_ref[gi], vc), 0.0)
        o_ref[r, :] = (u[r, :] * mixed).astype(BF16)


def _sgu(x, ln_g, ln_b, w_s, b_s, bsz, seq):
    t = x.shape[0]
    blk = SEQ_TILE
    bias = jnp.repeat(b_s.T.astype(F32), SGU_WIDTH // SGU_GROUPS, axis=1)
    fixed2 = lambda i: (0, 0)
    return pl.pallas_call(
        _sgu_kernel,
        grid=(t // blk,),
        in_specs=[pl.BlockSpec((blk, 2 * SGU_WIDTH), lambda i: (i, 0)),
                  pl.BlockSpec((1, SGU_WIDTH), fixed2),
                  pl.BlockSpec((1, SGU_WIDTH), fixed2),
                  pl.BlockSpec((SGU_GROUPS, SGU_CHUNK, SGU_CHUNK), lambda i: (0, 0, 0)),
                  pl.BlockSpec((SGU_CHUNK, SGU_WIDTH), fixed2)],
        out_specs=pl.BlockSpec((blk, SGU_WIDTH), lambda i: (i, 0)),
        out_shape=jax.ShapeDtypeStruct((t, SGU_WIDTH), BF16),
        compiler_params=_cparams(("parallel",)),
        name="sgu",
    )(x, ln_g.reshape(1, -1).astype(F32), ln_b.reshape(1, -1).astype(F32), w_s.astype(BF16), bias)


def _pool_kernel(seq, prev_ref, main_ref, next_ref, w_ref, scale_ref, o_ref):
    blk = main_ref.shape[0]
    n = blk + 2 * HALO
    xh = _with_halo(prev_ref, main_ref, next_ref)
    shift = lambda a, k: pltpu.roll(a, k % n, 0)
    s2 = shift(xh, 1) + xh
    s4 = shift(s2, 1) + shift(s2, -1)
    s8 = shift(s4, 2) + shift(s4, -2)
    s16 = shift(s8, 4) + shift(s8, -4)
    sums = (s2, s4, s8, s16)
    x = xh[HALO:HALO + blk, :]
    group = lax.broadcasted_iota(jnp.int32, (blk, GROUP_WIDTH), 1) // POOL_GW
    pos = lax.broadcasted_iota(jnp.int32, (blk, GROUP_WIDTH), 0) + pl.program_id(1) * blk
    wsum = jnp.zeros((blk, GROUP_WIDTH), F32)
    cnt = jnp.ones((blk, GROUP_WIDTH), F32)
    for gi, win in enumerate(POOL_WINDOWS):
        hi = jnp.minimum(pos + win // 2, seq)
        lo = jnp.maximum(pos - win // 2, 0)
        wsum = jnp.where(group == gi, sums[gi][HALO:HALO + blk, :], wsum)
        cnt = jnp.where(group == gi, (hi - lo).astype(F32), cnt)
    pooled = wsum / cnt - x
    o_ref[...] = (_dot(pooled, w_ref[...]) * scale_ref[...]).astype(BF16)


def _pool(x, w_pool, scale, bsz, seq):
    blk = SEQ_TILE
    nb = seq // blk
    x3 = x.reshape(bsz, seq, GROUP_WIDTH)
    prev, main, nxt = _halo_specs(blk, GROUP_WIDTH, nb)
    wbd = jax.scipy.linalg.block_diag(*[w_pool[g].astype(F32) for g in range(len(POOL_WINDOWS))])
    fixed = lambda b, i: (0, 0)
    out = pl.pallas_call(
        functools.partial(_pool_kernel, seq),
        grid=(bsz, nb),
        in_specs=[prev, main, nxt,
                  pl.BlockSpec((GROUP_WIDTH, GROUP_WIDTH), fixed),
                  pl.BlockSpec((1, GROUP_WIDTH), fixed)],
        out_specs=pl.BlockSpec((None, blk, GROUP_WIDTH), lambda b, i: (b, i, 0)),
        out_shape=jax.ShapeDtypeStruct((bsz, seq, GROUP_WIDTH), BF16),
        compiler_params=_cparams(("parallel", "parallel")),
        name="pool",
    )(x3, x3, x3, wbd, scale.reshape(1, -1).astype(F32))
    return out.reshape(bsz * seq, GROUP_WIDTH)


def _rms(x, g):
    return x * lax.rsqrt(jnp.mean(x * x, axis=-1, keepdims=True) + RMS_EPS) * g


def _mla_prep_kernel(x_ref, qg_ref, kvg_ref, wq_ref, wk_ref, wvt_ref, cos_ref, sin_ref, q_ref, k_ref, vt_ref):
    hw = MLA_HEADS * LANES
    cq = _rms(x_ref[:, 0:MLA_Q_RANK].astype(F32), qg_ref[...]).astype(BF16)
    ckv = _rms(x_ref[:, MLA_Q_RANK:MLA_Q_RANK + MLA_KV_RANK].astype(F32), kvg_ref[...]).astype(BF16)
    c0 = MLA_Q_RANK + MLA_KV_RANK
    cos = cos_ref[...]
    sin = sin_ref[...]
    k_rope = (x_ref[:, c0:c0 + LANES].astype(F32) * cos
              + x_ref[:, c0 + LANES:c0 + 2 * LANES].astype(F32) * sin)
    qa = _dot(cq, wq_ref[:, 0:hw])
    qb = _dot(cq, wq_ref[:, hw:2 * hw])
    kn = _dot(ckv, wk_ref[...])
    vt = _dot_nt(wvt_ref[...], ckv).astype(BF16)
    ones = jnp.ones((MLA_VROWS - MLA_V, vt.shape[1]), BF16)
    vt = jnp.concatenate([piece for h in range(MLA_HEADS)
                          for piece in (vt[h * MLA_V:(h + 1) * MLA_V, :], ones)], axis=0)
    for c in range(vt_ref.shape[0]):
        vt_ref[c] = vt[:, c * ATT_TK:(c + 1) * ATT_TK]
    for h in range(MLA_HEADS):
        c = slice(h * LANES, (h + 1) * LANES)
        q_ref[:, c] = (qa[:, c] * cos + qb[:, c] * sin).astype(BF16)
        k_ref[:, c] = (kn[:, c] + k_rope).astype(BF16)


def _rope_tables(seq):
    inv = ROPE_THETA ** (-jnp.arange(0, MLA_ROPE, 2, dtype=F32) / MLA_ROPE)
    ang = jnp.arange(seq, dtype=F32)[:, None] * inv[None, :]
    pad = LANES - MLA_NOPE - MLA_ROPE
    cos = jnp.concatenate([jnp.ones((seq, MLA_NOPE), F32), jnp.cos(ang), jnp.cos(ang), jnp.zeros((seq, pad), F32)], 1)
    sin = jnp.concatenate([jnp.zeros((seq, MLA_NOPE), F32), jnp.sin(ang), jnp.sin(ang), jnp.zeros((seq, pad), F32)], 1)
    return cos, sin


def _pack_mla_weights(w_uq, w_uk, w_uv):
    half = MLA_ROPE // 2
    scale = (MLA_NOPE + MLA_ROPE) ** -0.5 * np.log2(np.e)
    pad = LANES - MLA_NOPE - MLA_ROPE
    qa, qb, kn = [], [], []
    for h in range(MLA_HEADS):
        wq = w_uq[:, h, :].astype(F32) * scale
        rope = wq[:, MLA_NOPE:]
        rot = jnp.concatenate([-rope[:, half:], rope[:, :half]], axis=1)
        zq = lambda n: jnp.zeros((MLA_Q_RANK, n), F32)
        qa.append(jnp.concatenate([wq, zq(pad)], axis=1))
        qb.append(jnp.concatenate([zq(MLA_NOPE), rot, zq(pad)], axis=1))
        kn.append(jnp.concatenate([w_uk[:, h, :].astype(F32), jnp.zeros((MLA_KV_RANK, LANES - MLA_NOPE), F32)], axis=1))
    wq_packed = jnp.concatenate(qa + qb, axis=1).astype(BF16)
    wk_packed = jnp.concatenate(kn, axis=1).astype(BF16)
    wv_t = w_uv.reshape(MLA_KV_RANK, MLA_HEADS * MLA_V).T.astype(BF16)
    return wq_packed, wk_packed, wv_t


def _mla_prep(x, q_g, kv_g, wq_packed, wk_packed, wv_t, cos, sin, bsz, seq):
    t = x.shape[0]
    tm = ROW_TILE
    per_seq = seq // tm
    per_tile = tm // ATT_TK
    hw = MLA_HEADS * LANES
    row = lambda i: (i, 0)
    fixed = lambda i: (0, 0)
    table = lambda i: (i % per_seq, 0)
    return pl.pallas_call(
        _mla_prep_kernel,
        grid=(t // tm,),
        in_specs=[pl.BlockSpec((tm, 768), row),
                  pl.BlockSpec((1, MLA_Q_RANK), fixed),
                  pl.BlockSpec((1, MLA_KV_RANK), fixed),
                  pl.BlockSpec((MLA_Q_RANK, 2 * hw), fixed),
                  pl.BlockSpec((MLA_KV_RANK, hw), fixed),
                  pl.BlockSpec((GROUP_WIDTH, MLA_KV_RANK), fixed),
                  pl.BlockSpec((tm, LANES), table),
                  pl.BlockSpec((tm, LANES), table)],
        out_specs=[pl.BlockSpec((tm, hw), row),
                   pl.BlockSpec((tm, hw), row),
                   pl.BlockSpec((None, per_tile, MLA_HEADS * MLA_VROWS, ATT_TK),
                                lambda i: (i // per_seq, i % per_seq, 0, 0))],
        out_shape=[jax.ShapeDtypeStruct((t, hw), BF16),
                   jax.ShapeDtypeStruct((t, hw), BF16),
                   jax.ShapeDtypeStruct((bsz, seq // ATT_TK, MLA_HEADS * MLA_VROWS, ATT_TK), BF16)],
        compiler_params=_cparams(("parallel",)),
        name="mla_prep",
    )(x, q_g.reshape(1, -1).astype(F32), kv_g.reshape(1, -1).astype(F32), wq_packed, wk_packed, wv_t, cos, sin)


def _attn_kernel(q_ref, k_ref, vt_ref, o_ref):
    n_steps = vt_ref.shape[0]
    tq = q_ref.shape[0]
    tk = ATT_TK
    heads = range(q_ref.shape[1] // LANES)
    unroll = min(n_steps, ATT_UNROLL)
    qs = [q_ref[:, h * LANES:(h + 1) * LANES] for h in heads]

    def scores(h, j):
        r0 = pl.multiple_of(j * tk, tk)
        return _dot_nt(k_ref[pl.ds(r0, tk), h * LANES:(h + 1) * LANES], qs[h])

    def update(h, j, s, m, acc):
        m_new = jnp.maximum(m, jnp.max(s, axis=0, keepdims=True))
        p = jnp.exp2(s - m_new).astype(BF16)
        pv = _dot(vt_ref[j, h * MLA_VROWS:(h + 1) * MLA_VROWS, :], p)
        return m_new, acc * jnp.exp2(m - m_new) + pv

    def group(g, carry):
        base = g * unroll
        state = [list(carry[2 * h:2 * h + 2]) for h in heads]
        pending = [[scores(h, base + u) for h in heads] for u in range(min(ATT_LOOKAHEAD, unroll))]
        for u in range(unroll):
            s_cur = pending.pop(0)
            if u + ATT_LOOKAHEAD < unroll:
                pending.append([scores(h, base + u + ATT_LOOKAHEAD) for h in heads])
            for h in heads:
                state[h] = list(update(h, base + u, s_cur[h], *state[h]))
        return tuple(v for st in state for v in st)

    one = (jnp.full((1, tq), -jnp.inf, F32), jnp.zeros((MLA_VROWS, tq), F32))
    final = lax.fori_loop(0, n_steps // unroll, group, one * len(heads))
    accs = [final[2 * h + 1] for h in heads]
    out_t = jnp.concatenate([acc[:MLA_V, :] / acc[MLA_V:MLA_V + 1, :] for acc in accs], axis=0)
    o_ref[...] = out_t.T.astype(BF16)


def _attention(q, k, vt, bsz, seq):
    hw = MLA_HEADS * LANES
    q3, k3 = (a.reshape(bsz, seq, hw) for a in (q, k))
    tq = ATT_TQ
    hps = MLA_HEADS if seq // ATT_TK <= ATT_ALL_HEADS_MAX_STEPS else 2
    out = pl.pallas_call(
        _attn_kernel,
        grid=(bsz, MLA_HEADS // hps, seq // tq),
        in_specs=[pl.BlockSpec((None, tq, hps * LANES), lambda b, p, i: (b, i, p)),
                  pl.BlockSpec((None, seq, hps * LANES), lambda b, p, i: (b, 0, p)),
                  pl.BlockSpec((None, seq // ATT_TK, hps * MLA_VROWS, ATT_TK), lambda b, p, i: (b, 0, p, 0))],
        out_specs=pl.BlockSpec((None, tq, hps * MLA_V), lambda b, p, i: (b, i, p)),
        out_shape=jax.ShapeDtypeStruct((bsz, seq, GROUP_WIDTH), BF16),
        compiler_params=_cparams(("parallel", "parallel", "arbitrary"), 48),
        name="mla_attention",
    )(q3, k3, vt)
    return out.reshape(bsz * seq, GROUP_WIDTH)


def _out_proj_kernel(pre_ln, x_ref, lng_ref, lnb_ref, of_ref, ob_ref, z_ref, ng_ref, headmean_ref,
                     yb_ref, yc_ref, yd_ref, w_ref, g_ref, b_ref, o_ref):
    x = x_ref[...]
    if pre_ln:
        x = _layer_norm(x, lng_ref[...], lnb_ref[...])
    o = of_ref[...].astype(F32) + ob_ref[...].astype(F32)
    z = z_ref[...].astype(F32)
    ya = o * lax.rsqrt(_dot(o * o, headmean_ref[...]) + RMS_EPS) * ng_ref[...] * _silu(z)
    gw = GROUP_WIDTH
    mix = (_dot(ya.astype(BF16), w_ref[0:gw, :]) + _dot(yb_ref[...], w_ref[gw:2 * gw, :])
           + _dot(yc_ref[...], w_ref[2 * gw:3 * gw, :]) + _dot(yd_ref[...], w_ref[3 * gw:, :]))
    o_ref[...] = _layer_norm(DEEPNORM_ALPHA * x + mix, g_ref[...], b_ref[...])


def _out_proj(x, ln_in_g, ln_in_b, o_f, o_b, z, norm_g, y_b, y_c, y_d, w_out, ln_g, ln_b, pre_ln):
    t = x.shape[0]
    tm = ROW_TILE
    head = np.arange(GROUP_WIDTH) // DN_DK
    headmean = jnp.asarray((head[:, None] == head[None, :]).astype(np.float32) / DN_DK)
    ng = jnp.tile(norm_g.astype(F32), DN_HEADS).reshape(1, GROUP_WIDTH)
    row = lambda i: (i, 0)
    fixed = lambda i: (0, 0)
    quarter = pl.BlockSpec((tm, GROUP_WIDTH), row)
    vec = pl.BlockSpec((1, D_MODEL), fixed)
    return pl.pallas_call(
        functools.partial(_out_proj_kernel, pre_ln),
        grid=(t // tm,),
        in_specs=[pl.BlockSpec((tm, D_MODEL), row), vec, vec,
                  quarter, quarter, quarter,
                  pl.BlockSpec((1, GROUP_WIDTH), fixed),
                  pl.BlockSpec((GROUP_WIDTH, GROUP_WIDTH), fixed),
                  quarter, quarter, quarter,
                  pl.BlockSpec((D_MODEL, D_MODEL), fixed), vec, vec],
        out_specs=pl.BlockSpec((tm, D_MODEL), row),
        out_shape=jax.ShapeDtypeStruct((t, D_MODEL), F32),
        compiler_params=_cparams(("parallel",)),
        name="out_proj",
    )(x, ln_in_g, ln_in_b, o_f, o_b, z, ng, headmean, y_b, y_c, y_d, w_out.astype(BF16),
      ln_g.reshape(1, -1).astype(F32), ln_b.reshape(1, -1).astype(F32))


def _router_kernel(x_ref, whi_ref, wlo_ref, bias_ref, cls_ref, rank_ref, cnt_ref, carry_ref):
    tm = x_ref.shape[0]

    @pl.when(pl.program_id(0) == 0)
    def _():
        carry_ref[...] = jnp.zeros_like(carry_ref)

    x = x_ref[...]
    x_hi = x.astype(BF16)
    x_lo = (x - x_hi.astype(F32)).astype(BF16)
    logits = _dot_nt(whi_ref[...], x_hi) + (_dot_nt(whi_ref[...], x_lo) + _dot_nt(wlo_ref[...], x_hi))
    biased = _sigmoid(logits) + bias_ref[...]
    v = [biased[e:e + 1, :] for e in range(N_EXPERTS)]
    best = chosen = pair = None
    for g in range(N_EXPERT_GROUPS):
        m = v[g * EXPERTS_PER_GROUP:(g + 1) * EXPERTS_PER_GROUP]
        score = functools.reduce(jnp.maximum, [m[i] + m[j] for i, j in PAIRS])
        picked = []
        for e in range(EXPERTS_PER_GROUP):
            outranked = [(m[o] >= m[e]) if o < e else (m[o] > m[e]) for o in range(EXPERTS_PER_GROUP) if o != e]
            picked.append(functools.reduce(jnp.add, [c.astype(jnp.int32) for c in outranked]) < 2)
        first = jnp.where(picked[0], 0, jnp.where(picked[1], 1, 2))
        last = jnp.where(picked[3], 3, jnp.where(picked[2], 2, 1))
        pair_g = (first * (7 - first)) // 2 + (last - first - 1)
        if g == 0:
            best, chosen, pair = score, jnp.zeros_like(pair_g), pair_g
        else:
            better = score > best
            best = jnp.where(better, score, best)
            chosen = jnp.where(better, g, chosen)
            pair = jnp.where(better, pair_g, pair)
    cls_row = (chosen * len(PAIRS) + pair).astype(F32)
    onehot = (lax.broadcasted_iota(jnp.int32, (CLASS_ROWS, tm), 0).astype(F32) == cls_row)
    upto = (lax.broadcasted_iota(jnp.int32, (tm, tm), 0) <= lax.broadcasted_iota(jnp.int32, (tm, tm), 1))
    prefix = _dot(onehot.astype(BF16), upto.astype(BF16))
    carry = carry_ref[:, 0:1]
    rank = jnp.sum(jnp.where(onehot, prefix + carry, 0.0), axis=0, keepdims=True) - 1.0
    cls_ref[...] = jnp.broadcast_to(cls_row, (HALO, tm))
    rank_ref[...] = jnp.broadcast_to(rank, (HALO, tm))
    total = carry + prefix[:, tm - 1:tm]
    carry_ref[...] = jnp.broadcast_to(total, carry_ref.shape)
    cnt_ref[...] = jnp.broadcast_to(total, cnt_ref.shape)


def _router(x, router_w, router_bias):
    t = x.shape[0]
    tm = ROW_TILE
    w = router_w.astype(F32).T
    w_hi = w.astype(BF16)
    w_lo = (w - w_hi.astype(F32)).astype(BF16)
    bias = router_bias.astype(F32).reshape(N_EXPERTS, 1)
    fixed = lambda i: (0, 0)
    cls, rank, cnt = pl.pallas_call(
        _router_kernel,
        grid=(t // tm,),
        in_specs=[pl.BlockSpec((tm, D_MODEL), lambda i: (i, 0)),
                  pl.BlockSpec((N_EXPERTS, D_MODEL), fixed),
                  pl.BlockSpec((N_EXPERTS, D_MODEL), fixed),
                  pl.BlockSpec((N_EXPERTS, 1), fixed)],
        out_specs=[pl.BlockSpec((HALO, tm), lambda i: (0, i)),
                   pl.BlockSpec((HALO, tm), lambda i: (0, i)),
                   pl.BlockSpec((CLASS_ROWS, LANES), fixed)],
        out_shape=[jax.ShapeDtypeStruct((HALO, t), F32),
                   jax.ShapeDtypeStruct((HALO, t), F32),
                   jax.ShapeDtypeStruct((CLASS_ROWS, LANES), F32)],
        scratch_shapes=[pltpu.VMEM((CLASS_ROWS, LANES), F32)],
        compiler_params=_cparams(("arbitrary",)),
        name="router",
    )(x, w_hi, w_lo, bias)
    return cls[0].astype(jnp.int32), rank[0].astype(jnp.int32), cnt[:N_CLASSES, 0].astype(jnp.int32)


def _routing_tables(cls, rank, counts, n_tiles):
    tm = MOE_TILE
    tiles = (counts + tm - 1) // tm
    first_tile = jnp.cumsum(tiles) - tiles
    dest = first_tile[cls] * tm + rank
    tile_ids = jnp.arange(n_tiles, dtype=jnp.int32)
    tile_cls = jnp.sum((jnp.cumsum(tiles)[None, :] <= tile_ids[:, None]).astype(jnp.int32), axis=1)
    tile_cls = jnp.minimum(tile_cls, N_CLASSES - 1)
    used = (tile_ids < jnp.sum(tiles)).astype(jnp.int32)
    pair = np.asarray(PAIRS, np.int32)
    base = (tile_cls // len(PAIRS)) * EXPERTS_PER_GROUP
    e1 = base + jnp.asarray(pair[:, 0])[tile_cls % len(PAIRS)]
    e2 = base + jnp.asarray(pair[:, 1])[tile_cls % len(PAIRS)]
    return dest.astype(jnp.int32), e1.astype(jnp.int32), e2.astype(jnp.int32), used


def _dispatch_kernel(dest_ref, x_ref, zeros_ref, xs_ref, sem):
    del zeros_ref
    tr = x_ref.shape[0]

    def row_copy(r):
        return pltpu.make_async_copy(x_ref.at[pl.ds(r, 1), :], xs_ref.at[pl.ds(dest_ref[0, 0, r], 1), :], sem)

    for r in range(tr):
        row_copy(r).start()
    for r in range(tr):
        row_copy(r).wait()


def _dispatch(x, dest, slots):
    t = x.shape[0]
    tr = DMA_TILE
    dest3 = dest.reshape(t // tr, 1, tr)
    return pl.pallas_call(
        _dispatch_kernel,
        grid=(t // tr,),
        in_specs=[pl.BlockSpec((1, 1, tr), lambda i: (i, 0, 0), memory_space=pltpu.SMEM),
                  pl.BlockSpec((tr, D_MODEL), lambda i: (i, 0)),
                  pl.BlockSpec(memory_space=pl.ANY)],
        out_specs=pl.BlockSpec(memory_space=pl.ANY),
        out_shape=jax.ShapeDtypeStruct(slots.shape, F32),
        scratch_shapes=[pltpu.SemaphoreType.DMA],
        input_output_aliases={2: 0},
        compiler_params=_cparams(("arbitrary",)),
        name="moe_dispatch",
    )(dest3, x, slots)


def _moe_kernel(e1_ref, e2_ref, used_ref, x_ref, rw_ref, wg1_ref, wu1_ref, wd1_ref, wg2_ref, wu2_ref, wd2_ref, y_ref):
    i = pl.program_id(0)

    @pl.when(used_ref[i] == 1)
    def _():
        xb = x_ref[...].astype(BF16)
        scores = _sigmoid(_dot(xb, rw_ref[...]))
        lane = lax.broadcasted_iota(jnp.int32, scores.shape, 1)
        s1 = jnp.sum(jnp.where(lane == e1_ref[i], scores, 0.0), axis=-1, keepdims=True)
        s2 = jnp.sum(jnp.where(lane == e2_ref[i], scores, 0.0), axis=-1, keepdims=True)
        inv = 1.0 / (s1 + s2)
        h1 = _silu(_dot(xb, wg1_ref[...])) * _dot(xb, wu1_ref[...]) * (s1 * inv)
        h2 = _silu(_dot(xb, wg2_ref[...])) * _dot(xb, wu2_ref[...]) * (s2 * inv)
        y_ref[...] = _dot(h1.astype(BF16), wd1_ref[...]) + _dot(h2.astype(BF16), wd2_ref[...])

    @pl.when(used_ref[i] == 0)
    def _():
        y_ref[...] = jnp.zeros_like(y_ref)


def _moe(xs, e1, e2, used, router_w, w_gate, w_up, w_down):
    n_slots = xs.shape[0]
    tm = MOE_TILE
    rw = jnp.zeros((D_MODEL, LANES), BF16).at[:, :N_EXPERTS].set(router_w.astype(BF16))
    up = lambda sel: pl.BlockSpec((None, D_MODEL, EXPERT_FF), lambda i, e1, e2, u: ((e1, e2)[sel][i], 0, 0))
    down = lambda sel: pl.BlockSpec((None, EXPERT_FF, D_MODEL), lambda i, e1, e2, u: ((e1, e2)[sel][i], 0, 0))
    grid_spec = pltpu.PrefetchScalarGridSpec(
        num_scalar_prefetch=3,
        grid=(n_slots // tm,),
        in_specs=[pl.BlockSpec((tm, D_MODEL), lambda i, e1, e2, u: (i, 0)),
                  pl.BlockSpec((D_MODEL, LANES), lambda i, e1, e2, u: (0, 0)),
                  up(0), up(0), down(0), up(1), up(1), down(1)],
        out_specs=pl.BlockSpec((tm, D_MODEL), lambda i, e1, e2, u: (i, 0)),
    )
    return pl.pallas_call(
        _moe_kernel,
        grid_spec=grid_spec,
        out_shape=jax.ShapeDtypeStruct((n_slots, D_MODEL), F32),
        compiler_params=_cparams(("arbitrary",), 48),
        name="moe_experts",
    )(e1, e2, used, xs, rw, w_gate, w_up, w_down, w_gate, w_up, w_down)


def _combine_kernel(dest_ref, dest_next_ref, x_ref, ys_ref, g_ref, b_ref, o_ref, buf, sem):
    tr = x_ref.shape[0]
    i = pl.program_id(0)
    slot = i % 2

    def row_copy(idx_ref, r, s):
        return pltpu.make_async_copy(ys_ref.at[pl.ds(idx_ref[0, 0, r], 1), :], buf.at[s, pl.ds(r, 1), :], sem.at[s])

    @pl.when(i == 0)
    def _():
        for r in range(tr):
            row_copy(dest_ref, r, slot).start()

    @pl.when(i + 1 < pl.num_programs(0))
    def _():
        for r in range(tr):
            row_copy(dest_next_ref, r, 1 - slot).start()

    for r in range(tr):
        row_copy(dest_ref, r, slot).wait()
    o_ref[...] = _layer_norm(DEEPNORM_ALPHA * x_ref[...] + buf[slot], g_ref[...], b_ref[...])


def _combine(x, ys, dest, ln_g, ln_b):
    t = x.shape[0]
    tr = DMA_TILE
    n = t // tr
    dest3 = dest.reshape(n, 1, tr)
    fixed = lambda i: (0, 0)
    return pl.pallas_call(
        _combine_kernel,
        grid=(n,),
        in_specs=[pl.BlockSpec((1, 1, tr), lambda i: (i, 0, 0), memory_space=pltpu.SMEM),
                  pl.BlockSpec((1, 1, tr), lambda i: (jnp.minimum(i + 1, n - 1), 0, 0), memory_space=pltpu.SMEM),
                  pl.BlockSpec((tr, D_MODEL), lambda i: (i, 0)),
                  pl.BlockSpec(memory_space=pl.ANY),
                  pl.BlockSpec((1, D_MODEL), fixed),
                  pl.BlockSpec((1, D_MODEL), fixed)],
        out_specs=pl.BlockSpec((tr, D_MODEL), lambda i: (i, 0)),
        out_shape=jax.ShapeDtypeStruct((t, D_MODEL), F32),
        scratch_shapes=[pltpu.VMEM((2, tr, D_MODEL), F32), pltpu.SemaphoreType.DMA((2,))],
        compiler_params=_cparams(("arbitrary",)),
        name="moe_combine",
    )(dest3, dest3, x, ys, ln_g.reshape(1, -1).astype(F32), ln_b.reshape(1, -1).astype(F32))


def _trunk(x, p, slots):
    bsz, seq, d = x.shape
    t = bsz * seq
    x = x.reshape(t, d)
    cos, sin = _rope_tables(seq)
    n_tiles = t // MOE_TILE + N_CLASSES
    for l in range(DEPTH):
        first = l == 0
        qkv, z, sgu_in, pool_in, mla_in, misc = _in_proj(x, p["ln_in_g"], p["ln_in_b"], p["w_in"][l], first)
        qkvn, gb = _dn_prep(qkv, misc, p["dn_conv_w"][l], p["dn_a_log"][l], p["dn_dt_bias"][l], bsz, seq)
        o_f, o_b = _dn_main(qkvn, gb, bsz, seq)
        y_b = _sgu(sgu_in, p["sgu_ln_g"][l], p["sgu_ln_b"][l], p["sgu_w"][l], p["sgu_b"][l], bsz, seq)
        y_c = _pool(pool_in, p["pool_w"][l], p["pool_scale"][l], bsz, seq)
        q, k, vt = _mla_prep(mla_in, p["mla_q_norm_g"][l], p["mla_kv_norm_g"][l], p["wq"][l], p["wk"][l],
                             p["wvt"][l], cos, sin, bsz, seq)
        y_d = _attention(q, k, vt, bsz, seq)
        x = _out_proj(x, p["ln_in_g"], p["ln_in_b"], o_f.reshape(t, -1), o_b.reshape(t, -1), z,
                      p["dn_norm_g"][l], y_b, y_c, y_d, p["w_out"][l], p["ln1_g"][l], p["ln1_b"][l], first)
        cls, rank, counts = _router(x, p["router_w"], p["router_bias"])
        dest, e1, e2, used = _routing_tables(cls, rank, counts, n_tiles)
        if slots is None or slots.shape[0] != n_tiles * MOE_TILE:
            slots = jnp.zeros((n_tiles * MOE_TILE, D_MODEL), F32)
        slots = _dispatch(x, dest, slots)
        ys = _moe(slots, e1, e2, used, p["router_w"], p["moe_w_gate"][l], p["moe_w_up"][l], p["moe_w_down"][l])
        x = _combine(x, ys, dest, p["ln2_g"][l], p["ln2_b"][l])
    return x.reshape(bsz, seq, d), slots


def kernel(x_prompt, x_sample, ln_in_g, ln_in_b, w_in, dn_conv_w, dn_a_log, dn_dt_bias, dn_norm_g, sgu_ln_g, sgu_ln_b, sgu_w, sgu_b, pool_w, pool_scale, mla_q_norm_g, mla_kv_norm_g, mla_w_uq, mla_w_uk, mla_w_uv, w_out, ln1_g, ln1_b, router_w, router_bias, moe_w_gate, moe_w_up, moe_w_down, ln2_g, ln2_b):
    packed_mla = [_pack_mla_weights(mla_w_uq[l], mla_w_uk[l], mla_w_uv[l]) for l in range(DEPTH)]
    p = dict(
        ln_in_g=ln_in_g.reshape(1, -1).astype(F32), ln_in_b=ln_in_b.reshape(1, -1).astype(F32),
        w_in=[_pack_in_weights(w_in[l]) for l in range(DEPTH)],
        dn_conv_w=dn_conv_w, dn_a_log=dn_a_log, dn_dt_bias=dn_dt_bias, dn_norm_g=dn_norm_g,
        sgu_ln_g=sgu_ln_g, sgu_ln_b=sgu_ln_b, sgu_w=sgu_w, sgu_b=sgu_b,
        pool_w=pool_w, pool_scale=pool_scale,
        mla_q_norm_g=mla_q_norm_g, mla_kv_norm_g=mla_kv_norm_g,
        wq=[w[0] for w in packed_mla], wk=[w[1] for w in packed_mla], wvt=[w[2] for w in packed_mla],
        w_out=w_out, ln1_g=ln1_g, ln1_b=ln1_b, router_w=router_w, router_bias=router_bias,
        moe_w_gate=moe_w_gate.astype(BF16), moe_w_up=moe_w_up.astype(BF16), moe_w_down=moe_w_down.astype(BF16),
        ln2_g=ln2_g, ln2_b=ln2_b,
    )
    y_prompt, slots = _trunk(x_prompt, p, None)
    y_sample, _ = _trunk(x_sample, p, slots)
    return y_prompt, y_sample
```

```python
import functools

import numpy as np
import jax
import jax.numpy as jnp
from jax import lax
from jax.experimental import pallas as pl
from jax.experimental.pallas import tpu as pltpu

F32 = jnp.float32
BF16 = jnp.bfloat16

D_MODEL = 1024
DEPTH = 2
GROUP_WIDTH = D_MODEL // 4
DN_HEADS = 4
DN_DK = GROUP_WIDTH // DN_HEADS
DN_CONV = 5
DN_CHUNK = 64
DN_GROUP = 4
SGU_GROUPS = 4
SGU_CHUNK = 128
SGU_WIDTH = GROUP_WIDTH
POOL_WINDOWS = (2, 4, 8, 16)
POOL_GW = GROUP_WIDTH // len(POOL_WINDOWS)
MLA_HEADS = 4
MLA_NOPE = 64
MLA_ROPE = 32
MLA_V = GROUP_WIDTH // MLA_HEADS
MLA_VROWS = MLA_V + 16
MLA_Q_RANK = 384
MLA_KV_RANK = 128
ROPE_THETA = 10000.0
N_EXPERTS = 16
N_EXPERT_GROUPS = 4
EXPERTS_PER_GROUP = N_EXPERTS // N_EXPERT_GROUPS
EXPERT_FF = 512
DEEPNORM_ALPHA = (2 * DEPTH) ** 0.25
LN_EPS = 1e-5
RMS_EPS = 1e-6

LANES = 128
HALO = 8

PAIRS = ((0, 1), (0, 2), (0, 3), (1, 2), (1, 3), (2, 3))
N_CLASSES = N_EXPERT_GROUPS * len(PAIRS)
CLASS_ROWS = 32

C_QKV = 0
C_Z = 768
C_SGU = 1024
C_POOL = 1536
C_MLA = 1792
C_MISC = 2560
IN_PACKED = 2688

ROW_TILE = 512
SEQ_TILE = 512
DN_BLOCK = 1024
MOE_TILE = 256
DMA_TILE = 512
ATT_TQ = 256
ATT_TK = 256
ATT_UNROLL = 32
ATT_ALL_HEADS_MAX_STEPS = 8
ATT_LOOKAHEAD = 3


def _cparams(sem, vmem_mb=None):
    kw = dict(dimension_semantics=sem)
    if vmem_mb is not None:
        kw["vmem_limit_bytes"] = vmem_mb * 1024 * 1024
    return pltpu.CompilerParams(**kw)


def _layer_norm(x, g, b):
    mu = jnp.mean(x, axis=-1, keepdims=True)
    xc = x - mu
    var = jnp.mean(xc * xc, axis=-1, keepdims=True)
    return xc * lax.rsqrt(var + LN_EPS) * g + b


def _sigmoid(x):
    return 1.0 / (1.0 + jnp.exp(-x))


def _silu(x):
    return x * _sigmoid(x)


def _dot(a, b):
    return jnp.dot(a, b, preferred_element_type=F32)


def _dot_nt(a, b):
    return lax.dot_general(a, b, (((1,), (1,)), ((), ())), preferred_element_type=F32)


def _dot_tn(a, b):
    return lax.dot_general(a, b, (((0,), (0,)), ((), ())), preferred_element_type=F32)


def _in_proj_kernel(pre_ln, x_ref, g_ref, b_ref, w_ref,
                    qkv_ref, z_ref, sgu_ref, pool_ref, mla_ref, misc_ref):
    x = x_ref[...]
    if pre_ln:
        x = _layer_norm(x, g_ref[...], b_ref[...])
    xb = x.astype(BF16)
    qkv_ref[...] = _dot(xb, w_ref[:, C_QKV:C_Z]).astype(BF16)
    z_ref[...] = _dot(xb, w_ref[:, C_Z:C_SGU]).astype(BF16)
    sgu_ref[...] = _dot(xb, w_ref[:, C_SGU:C_POOL]).astype(BF16)
    pool_ref[...] = _dot(xb, w_ref[:, C_POOL:C_MLA]).astype(BF16)
    mla_ref[...] = _dot(xb, w_ref[:, C_MLA:C_MISC]).astype(BF16)
    misc_ref[...] = _dot(xb, w_ref[:, C_MISC:IN_PACKED])


def _in_proj(x, ln_g, ln_b, w_packed, pre_ln):
    t = x.shape[0]
    tm = ROW_TILE
    widths = (768, 256, 512, 256, 768, 128)
    dtypes = (BF16, BF16, BF16, BF16, BF16, F32)
    row = lambda i: (i, 0)
    fixed = lambda i: (0, 0)
    return pl.pallas_call(
        functools.partial(_in_proj_kernel, pre_ln),
        grid=(t // tm,),
        in_specs=[pl.BlockSpec((tm, D_MODEL), row),
                  pl.BlockSpec((1, D_MODEL), fixed),
                  pl.BlockSpec((1, D_MODEL), fixed),
                  pl.BlockSpec((D_MODEL, IN_PACKED), fixed)],
        out_specs=[pl.BlockSpec((tm, w), row) for w in widths],
        out_shape=[jax.ShapeDtypeStruct((t, w), d) for w, d in zip(widths, dtypes)],
        compiler_params=_cparams(("parallel",), 48),
        name="in_proj",
    )(x, ln_g, ln_b, w_packed)


def _pack_in_weights(w):
    d = w.shape[0]
    zeros = lambda n: jnp.zeros((d, n), w.dtype)
    qkv, z = w[:, 0:768], w[:, 768:1024]
    bb, aa = w[:, 1024:1032], w[:, 1032:1040]
    sgu, pool = w[:, 1040:1552], w[:, 1552:1808]
    cq, ckv, kr = w[:, 1808:2192], w[:, 2192:2320], w[:, 2320:2352]
    half = MLA_ROPE // 2
    kr_rot = jnp.concatenate([-kr[:, half:], kr[:, :half]], axis=1)
    rope_tile = jnp.concatenate([zeros(MLA_NOPE), kr, zeros(LANES - MLA_NOPE - MLA_ROPE)], axis=1)
    rot_tile = jnp.concatenate([zeros(MLA_NOPE), kr_rot, zeros(LANES - MLA_NOPE - MLA_ROPE)], axis=1)
    misc = jnp.concatenate([bb, aa, zeros(LANES - 16)], axis=1)
    return jnp.concatenate([qkv, z, sgu, pool, cq, ckv, rope_tile, rot_tile, misc], axis=1).astype(BF16)


def _halo_specs(blk, width, n_seq_blocks):
    per = blk // HALO
    last = n_seq_blocks * per - 1
    main = pl.BlockSpec((None, blk, width), lambda b, i: (b, i, 0))
    prev = pl.BlockSpec((None, HALO, width), lambda b, i: (b, jnp.maximum(i * per - 1, 0), 0))
    nxt = pl.BlockSpec((None, HALO, width), lambda b, i: (b, jnp.minimum((i + 1) * per, last), 0))
    return prev, main, nxt


def _with_halo(prev_ref, main_ref, next_ref):
    i = pl.program_id(1)
    n = pl.num_programs(1)
    prev = jnp.where(i > 0, prev_ref[...].astype(F32), 0.0)
    nxt = jnp.where(i < n - 1, next_ref[...].astype(F32), 0.0)
    return jnp.concatenate([prev, main_ref[...].astype(F32), nxt], axis=0)


def _dn_prep_kernel(prev_ref, main_ref, next_ref, misc_ref, convw_ref, decay_a_ref, dtb_ref, headsum_ref,
                    qkvn_ref, gb_ref):
    blk = main_ref.shape[0]
    xh = _with_halo(prev_ref, main_ref, next_ref)
    acc = jnp.zeros((blk, 3 * GROUP_WIDTH), F32)
    n = blk + 2 * HALO
    for j in range(DN_CONV):
        tap = xh if j == DN_CONV // 2 else pltpu.roll(xh, (DN_CONV // 2 - j) % n, 0)
        acc = acc + tap[HALO:HALO + blk, :] * convw_ref[j:j + 1, :]
    x = _silu(acc)
    q = x[:, 0:GROUP_WIDTH]
    k = x[:, GROUP_WIDTH:2 * GROUP_WIDTH]
    v = x[:, 2 * GROUP_WIDTH:]
    hs = headsum_ref[...]
    q = q * lax.rsqrt(_dot(q * q, hs) + RMS_EPS) * (DN_DK ** -0.5)
    k = k * lax.rsqrt(_dot(k * k, hs) + RMS_EPS)
    qkvn_ref[:, 0:GROUP_WIDTH] = q.astype(BF16)
    qkvn_ref[:, GROUP_WIDTH:2 * GROUP_WIDTH] = k.astype(BF16)
    qkvn_ref[:, 2 * GROUP_WIDTH:] = v.astype(BF16)
    m = misc_ref[...]
    lane = lax.broadcasted_iota(jnp.int32, m.shape, 1)
    xs = m + dtb_ref[...]
    softplus = jnp.maximum(xs, 0.0) + jnp.log1p(jnp.exp(-jnp.abs(xs)))
    gb = jnp.where(lane < 2 * DN_HEADS, _sigmoid(m), -decay_a_ref[...] * softplus)
    gb_ref[...] = jnp.where(lane < 4 * DN_HEADS, gb, 0.0)


def _dn_prep(qkv, misc, conv_w, a_log, dt_bias, bsz, seq):
    blk = SEQ_TILE
    nb = seq // blk
    qkv3 = qkv.reshape(bsz, seq, 3 * GROUP_WIDTH)
    misc3 = misc.reshape(bsz, seq, LANES)
    prev, main, nxt = _halo_specs(blk, 3 * GROUP_WIDTH, nb)
    convw = jnp.zeros((HALO, 3 * GROUP_WIDTH), F32).at[:DN_CONV].set(conv_w.astype(F32))
    lane_vec = lambda v: jnp.zeros((1, LANES), F32).at[0, 2 * DN_HEADS:4 * DN_HEADS].set(v.reshape(-1).astype(F32))
    decay_a = lane_vec(jnp.exp(a_log.astype(F32)))
    dtb = lane_vec(dt_bias)
    head = np.arange(GROUP_WIDTH) // DN_DK
    headsum = jnp.asarray((head[:, None] == head[None, :]).astype(np.float32))
    fixed = lambda b, i: (0, 0)
    qkvn, gb = pl.pallas_call(
        _dn_prep_kernel,
        grid=(bsz, nb),
        in_specs=[prev, main, nxt,
                  pl.BlockSpec((None, blk, LANES), lambda b, i: (b, i, 0)),
                  pl.BlockSpec((HALO, 3 * GROUP_WIDTH), fixed),
                  pl.BlockSpec((1, LANES), fixed),
                  pl.BlockSpec((1, LANES), fixed),
                  pl.BlockSpec((GROUP_WIDTH, GROUP_WIDTH), fixed)],
        out_specs=[pl.BlockSpec((None, blk, 3 * GROUP_WIDTH), lambda b, i: (b, i, 0)),
                   pl.BlockSpec((None, blk, LANES), lambda b, i: (b, i, 0))],
        out_shape=[jax.ShapeDtypeStruct((bsz, seq, 3 * GROUP_WIDTH), BF16),
                   jax.ShapeDtypeStruct((bsz, seq, LANES), F32)],
        compiler_params=_cparams(("parallel", "parallel")),
        name="dn_prep",
    )(qkv3, qkv3, qkv3, misc3, convw, decay_a, dtb, headsum)
    return qkvn, gb


def _unit_triangular_inverses(mats, rows, cols, lowers):
    c = mats[0].shape[0]
    eye = (rows == cols).astype(F32)

    def level_mask(lower, s):
        hi, lo = (rows, cols) if lower else (cols, rows)
        return ((hi // s) == (lo // s) + 1) & ((hi // (2 * s)) == (lo // (2 * s)))

    ts = [eye - jnp.where(level_mask(lower, 1), a, 0.0) for a, lower in zip(mats, lowers)]
    s = 2
    while s < c:
        tc = [_dot(t, jnp.where(level_mask(lower, s), a, 0.0)) for t, a, lower in zip(ts, mats, lowers)]
        yield
        ts = [t - _dot(x, t) for t, x in zip(ts, tc)]
        yield
        s *= 2
    return ts


def _interleave(first, second):
    results = [None, None]
    live = [first, second]
    while any(g is not None for g in live):
        for i, g in enumerate(live):
            if g is not None:
                try:
                    next(g)
                except StopIteration as stop:
                    results[i] = stop.value
                    live[i] = None
    return results


def _dn_main_kernel(qkv_f_ref, gb_f_ref, qkv_b_ref, gb_b_ref, o_f_ref, o_b_ref, state_ref):
    blk = qkv_f_ref.shape[0]
    nc = blk // DN_CHUNK
    c_ = DN_CHUNK

    @pl.when(pl.program_id(1) == 0)
    def _():
        state_ref[...] = jnp.zeros_like(state_ref)

    rows = lax.broadcasted_iota(jnp.int32, (c_, c_), 0)
    cols = lax.broadcasted_iota(jnp.int32, (c_, c_), 1)
    tri_lo = (rows >= cols).astype(F32)
    tri_up = (rows <= cols).astype(F32)
    n_rec = 2 * DN_HEADS
    lowers = [i < DN_HEADS for i in range(n_rec)]

    def transpose_chunk(m):
        return jnp.concatenate([m, jnp.zeros((LANES - c_, LANES), F32)], axis=0).T[:, :c_]

    lane = lax.broadcasted_iota(jnp.int32, (c_, LANES), 1)

    def prepare(gi):
        r0s, qs, qds, ks, vs, kdts, gams, b_rows, gam_rows, e_tots = [], [], [], [], [], [], [], [], [], []
        for g in range(DN_GROUP):
            ci = gi * DN_GROUP + g
            for d in range(2):
                lower = d == 0
                qkv_ref, gb_ref = (qkv_f_ref, gb_f_ref) if lower else (qkv_b_ref, gb_b_ref)
                r0 = (ci if lower else nc - 1 - ci) * c_
                r0s.append(r0)
                x = qkv_ref[r0:r0 + c_, :].astype(F32)
                gbt = gb_ref[r0:r0 + c_, :]
                incl = (rows >= cols) if lower else (rows <= cols)
                gc = jnp.dot(tri_lo if lower else tri_up, gbt, precision=lax.Precision.HIGHEST,
                             preferred_element_type=F32)
                stats_t = transpose_chunk(jnp.where(lane < n_rec, gbt, gc))
                beta_t = stats_t[0:n_rec, :]
                gc_t = stats_t[n_rec:2 * n_rec, :]
                g_last_t = gc_t[:, c_ - 1:c_] if lower else gc_t[:, 0:1]
                e_gc_t = jnp.exp(gc_t)
                e_rest_t = jnp.exp(g_last_t - gc_t)
                e_tot = jnp.exp(gc[c_ - 1:c_, :] if lower else gc[0:1, :])
                k_t = [transpose_chunk(x[:, GROUP_WIDTH + j * LANES:GROUP_WIDTH + (j + 1) * LANES])
                       for j in range(GROUP_WIDTH // LANES)]
                for h in range(DN_HEADS):
                    lb = DN_HEADS * d + h
                    lg = n_rec + lb
                    b_row = beta_t[lb:lb + 1, :]
                    gc_col = jnp.broadcast_to(gc[:, lg:lg + 1], (c_, c_))
                    diff = gc_col - gc_t[lb:lb + 1, :]
                    half = (h * DN_DK) % LANES
                    qs.append(x[:, h * DN_DK:(h + 1) * DN_DK])
                    qds.append(qs[-1] * jnp.exp(gc_col))
                    ks.append(x[:, GROUP_WIDTH + h * DN_DK:GROUP_WIDTH + (h + 1) * DN_DK])
                    vs.append(x[:, 2 * GROUP_WIDTH + h * DN_DK:2 * GROUP_WIDTH + (h + 1) * DN_DK])
                    kdts.append(k_t[(h * DN_DK) // LANES][half:half + DN_DK, :] * (e_rest_t[lb:lb + 1, :] * b_row))
                    gams.append(jnp.where(incl, jnp.exp(jnp.where(incl, diff, 0.0)), 0.0))
                    b_rows.append(b_row)
                    gam_rows.append(e_gc_t[lb:lb + 1, :])
                    e_tots.append(e_tot[:, lg:lg + 1])
        every = range(DN_GROUP * n_rec)
        low = [lowers[i % n_rec] for i in every]
        gkk = [_dot_nt(ks[i], ks[i]) for i in every]
        yield
        gqk = [_dot_nt(qs[i], ks[i]) for i in every]
        yield
        a = [jnp.where((rows > cols) if low[i] else (rows < cols), gkk[i] * gams[i], 0.0) * b_rows[i] for i in every]
        qk = [gqk[i] * gams[i] * b_rows[i] for i in every]
        t = yield from _unit_triangular_inverses(a, rows, cols, low)
        us = [_dot(t[i], vs[i]) for i in every]
        yield
        ws = [_dot(t[i] * gam_rows[i], ks[i]) for i in every]
        yield
        return r0s, qds, ws, us, qk, kdts, e_tots

    def recur(prepared, states):
        r0s, qds, ws, us, qk, kdts, e_tots = prepared
        for g in range(DN_GROUP):
            rec = range(g * n_rec, (g + 1) * n_rec)
            qws = [_dot(jnp.concatenate([qds[i], ws[i]], axis=0), states[i % n_rec]) for i in rec]
            yield
            v_new = [us[i] - x[c_:, :] for i, x in zip(rec, qws)]
            outs = [x[:c_, :] + _dot(qk[i], vn) for i, x, vn in zip(rec, qws, v_new)]
            upd = [_dot(kdts[i], vn) for i, vn in zip(rec, v_new)]
            yield
            states = [states[i % n_rec] * e_tots[i] + u for i, u in zip(rec, upd)]
            r_f, r_b = r0s[2 * g], r0s[2 * g + 1]
            o_f_ref[r_f:r_f + c_, :] = jnp.concatenate(outs[:DN_HEADS], axis=1).astype(BF16)
            o_b_ref[r_b:r_b + c_, :] = jnp.concatenate(outs[DN_HEADS:], axis=1).astype(BF16)
        return states

    n_groups = nc // DN_GROUP
    states = [state_ref[r] for r in range(n_rec)]
    _, prepared = _interleave(iter(()), prepare(0))
    for gi in range(n_groups):
        following = prepare(gi + 1) if gi + 1 < n_groups else iter(())
        states, prepared = _interleave(recur(prepared, states), following)
    for r in range(n_rec):
        state_ref[r] = states[r]


def _dn_main(qkvn, gb, bsz, seq):
    blk = min(DN_BLOCK, seq)
    assert seq % blk == 0 and blk % (DN_GROUP * DN_CHUNK) == 0
    nb = seq // blk
    fwd = lambda b, i: (b, i, 0)
    bwd = lambda b, i: (b, nb - 1 - i, 0)
    return pl.pallas_call(
        _dn_main_kernel,
        grid=(bsz, nb),
        in_specs=[pl.BlockSpec((None, blk, 3 * GROUP_WIDTH), fwd),
                  pl.BlockSpec((None, blk, LANES), fwd),
                  pl.BlockSpec((None, blk, 3 * GROUP_WIDTH), bwd),
                  pl.BlockSpec((None, blk, LANES), bwd)],
        out_specs=[pl.BlockSpec((None, blk, GROUP_WIDTH), fwd),
                   pl.BlockSpec((None, blk, GROUP_WIDTH), bwd)],
        out_shape=[jax.ShapeDtypeStruct((bsz, seq, GROUP_WIDTH), BF16)] * 2,
        scratch_shapes=[pltpu.VMEM((2 * DN_HEADS, DN_DK, DN_DK), F32)],
        compiler_params=_cparams(("parallel", "arbitrary")),
        name="dn_main",
    )(qkvn, gb, qkvn, gb)


def _sgu_kernel(x_ref, g_ref, b_ref, w_ref, bias_ref, o_ref):
    blk = x_ref.shape[0]
    x = x_ref[...].astype(F32)
    x = 0.5 * x * (1.0 + jnp.tanh(np.sqrt(2.0 / np.pi) * (x + 0.044715 * x * x * x)))
    u = x[:, :SGU_WIDTH]
    v = _layer_norm(x[:, SGU_WIDTH:], g_ref[...], b_ref[...]).astype(BF16)
    group = lax.broadcasted_iota(jnp.int32, (SGU_CHUNK, SGU_WIDTH), 1) // (SGU_WIDTH // SGU_GROUPS)
    for c in range(blk // SGU_CHUNK):
        r = slice(c * SGU_CHUNK, (c + 1) * SGU_CHUNK)
        vc = v[r, :]
        mixed = bias_ref[...]
        for gi in range(SGU_GROUPS):
            mixed = mixed + jnp.where(group == gi, _dot(w_ref[gi], vc), 0.0)
        o_ref[r, :] = (u[r, :] * mixed).astype(BF16)


def _sgu(x, ln_g, ln_b, w_s, b_s, bsz, seq):
    t = x.shape[0]
    blk = SEQ_TILE
    bias = jnp.repeat(b_s.T.astype(F32), SGU_WIDTH // SGU_GROUPS, axis=1)
    fixed2 = lambda i: (0, 0)
    return pl.pallas_call(
        _sgu_kernel,
        grid=(t // blk,),
        in_specs=[pl.BlockSpec((blk, 2 * SGU_WIDTH), lambda i: (i, 0)),
                  pl.BlockSpec((1, SGU_WIDTH), fixed2),
                  pl.BlockSpec((1, SGU_WIDTH), fixed2),
                  pl.BlockSpec((SGU_GROUPS, SGU_CHUNK, SGU_CHUNK), lambda i: (0, 0, 0)),
                  pl.BlockSpec((SGU_CHUNK, SGU_WIDTH), fixed2)],
        out_specs=pl.BlockSpec((blk, SGU_WIDTH), lambda i: (i, 0)),
        out_shape=jax.ShapeDtypeStruct((t, SGU_WIDTH), BF16),
        compiler_params=_cparams(("parallel",)),
        name="sgu",
    )(x, ln_g.reshape(1, -1).astype(F32), ln_b.reshape(1, -1).astype(F32), w_s.astype(BF16), bias)


def _pool_kernel(seq, prev_ref, main_ref, next_ref, w_ref, scale_ref, o_ref):
    blk = main_ref.shape[0]
    n = blk + 2 * HALO
    xh = _with_halo(prev_ref, main_ref, next_ref)
    shift = lambda a, k: pltpu.roll(a, k % n, 0)
    s2 = shift(xh, 1) + xh
    s4 = shift(s2, 1) + shift(s2, -1)
    s8 = shift(s4, 2) + shift(s4, -2)
    s16 = shift(s8, 4) + shift(s8, -4)
    sums = (s2, s4, s8, s16)
    x = xh[HALO:HALO + blk, :]
    group = lax.broadcasted_iota(jnp.int32, (blk, GROUP_WIDTH), 1) // POOL_GW
    pos = lax.broadcasted_iota(jnp.int32, (blk, GROUP_WIDTH), 0) + pl.program_id(1) * blk
    wsum = jnp.zeros((blk, GROUP_WIDTH), F32)
    cnt = jnp.ones((blk, GROUP_WIDTH), F32)
    for gi, win in enumerate(POOL_WINDOWS):
        hi = jnp.minimum(pos + win // 2, seq)
        lo = jnp.maximum(pos - win // 2, 0)
        wsum = jnp.where(group == gi, sums[gi][HALO:HALO + blk, :], wsum)
        cnt = jnp.where(group == gi, (hi - lo).astype(F32), cnt)
    pooled = wsum / cnt - x
    o_ref[...] = (_dot(pooled, w_ref[...]) * scale_ref[...]).astype(BF16)


def _pool(x, w_pool, scale, bsz, seq):
    blk = SEQ_TILE
    nb = seq // blk
    x3 = x.reshape(bsz, seq, GROUP_WIDTH)
    prev, main, nxt = _halo_specs(blk, GROUP_WIDTH, nb)
    wbd = jax.scipy.linalg.block_diag(*[w_pool[g].astype(F32) for g in range(len(POOL_WINDOWS))])
    fixed = lambda b, i: (0, 0)
    out = pl.pallas_call(
        functools.partial(_pool_kernel, seq),
        grid=(bsz, nb),
        in_specs=[prev, main, nxt,
                  pl.BlockSpec((GROUP_WIDTH, GROUP_WIDTH), fixed),
                  pl.BlockSpec((1, GROUP_WIDTH), fixed)],
        out_specs=pl.BlockSpec((None, blk, GROUP_WIDTH), lambda b, i: (b, i, 0)),
        out_shape=jax.ShapeDtypeStruct((bsz, seq, GROUP_WIDTH), BF16),
        compiler_params=_cparams(("parallel", "parallel")),
        name="pool",
    )(x3, x3, x3, wbd, scale.reshape(1, -1).astype(F32))
    return out.reshape(bsz * seq, GROUP_WIDTH)


def _rms(x, g):
    return x * lax.rsqrt(jnp.mean(x * x, axis=-1, keepdims=True) + RMS_EPS) * g


def _mla_prep_kernel(x_ref, qg_ref, kvg_ref, wq_ref, wk_ref, wvt_ref, cos_ref, sin_ref, q_ref, k_ref, vt_ref):
    hw = MLA_HEADS * LANES
    cq = _rms(x_ref[:, 0:MLA_Q_RANK].astype(F32), qg_ref[...]).astype(BF16)
    ckv = _rms(x_ref[:, MLA_Q_RANK:MLA_Q_RANK + MLA_KV_RANK].astype(F32), kvg_ref[...]).astype(BF16)
    c0 = MLA_Q_RANK + MLA_KV_RANK
    cos = cos_ref[...]
    sin = sin_ref[...]
    k_rope = (x_ref[:, c0:c0 + LANES].astype(F32) * cos
              + x_ref[:, c0 + LANES:c0 + 2 * LANES].astype(F32) * sin)
    qa = _dot(cq, wq_ref[:, 0:hw])
    qb = _dot(cq, wq_ref[:, hw:2 * hw])
    kn = _dot(ckv, wk_ref[...])
    vt = _dot_nt(wvt_ref[...], ckv).astype(BF16)
    ones = jnp.ones((MLA_VROWS - MLA_V, vt.shape[1]), BF16)
    vt = jnp.concatenate([piece for h in range(MLA_HEADS)
                          for piece in (vt[h * MLA_V:(h + 1) * MLA_V, :], ones)], axis=0)
    for c in range(vt_ref.shape[0]):
        vt_ref[c] = vt[:, c * ATT_TK:(c + 1) * ATT_TK]
    for h in range(MLA_HEADS):
        c = slice(h * LANES, (h + 1) * LANES)
        q_ref[:, c] = (qa[:, c] * cos + qb[:, c] * sin).astype(BF16)
        k_ref[:, c] = (kn[:, c] + k_rope).astype(BF16)


def _rope_tables(seq):
    inv = ROPE_THETA ** (-jnp.arange(0, MLA_ROPE, 2, dtype=F32) / MLA_ROPE)
    ang = jnp.arange(seq, dtype=F32)[:, None] * inv[None, :]
    pad = LANES - MLA_NOPE - MLA_ROPE
    cos = jnp.concatenate([jnp.ones((seq, MLA_NOPE), F32), jnp.cos(ang), jnp.cos(ang), jnp.zeros((seq, pad), F32)], 1)
    sin = jnp.concatenate([jnp.zeros((seq, MLA_NOPE), F32), jnp.sin(ang), jnp.sin(ang), jnp.zeros((seq, pad), F32)], 1)
    return cos, sin


def _pack_mla_weights(w_uq, w_uk, w_uv):
    half = MLA_ROPE // 2
    scale = (MLA_NOPE + MLA_ROPE) ** -0.5 * np.log2(np.e)
    pad = LANES - MLA_NOPE - MLA_ROPE
    qa, qb, kn = [], [], []
    for h in range(MLA_HEADS):
        wq = w_uq[:, h, :].astype(F32) * scale
        rope = wq[:, MLA_NOPE:]
        rot = jnp.concatenate([-rope[:, half:], rope[:, :half]], axis=1)
        zq = lambda n: jnp.zeros((MLA_Q_RANK, n), F32)
        qa.append(jnp.concatenate([wq, zq(pad)], axis=1))
        qb.append(jnp.concatenate([zq(MLA_NOPE), rot, zq(pad)], axis=1))
        kn.append(jnp.concatenate([w_uk[:, h, :].astype(F32), jnp.zeros((MLA_KV_RANK, LANES - MLA_NOPE), F32)], axis=1))
    wq_packed = jnp.concatenate(qa + qb, axis=1).astype(BF16)
    wk_packed = jnp.concatenate(kn, axis=1).astype(BF16)
    wv_t = w_uv.reshape(MLA_KV_RANK, MLA_HEADS * MLA_V).T.astype(BF16)
    return wq_packed, wk_packed, wv_t


def _mla_prep(x, q_g, kv_g, wq_packed, wk_packed, wv_t, cos, sin, bsz, seq):
    t = x.shape[0]
    tm = ROW_TILE
    per_seq = seq // tm
    per_tile = tm // ATT_TK
    hw = MLA_HEADS * LANES
    row = lambda i: (i, 0)
    fixed = lambda i: (0, 0)
    table = lambda i: (i % per_seq, 0)
    return pl.pallas_call(
        _mla_prep_kernel,
        grid=(t // tm,),
        in_specs=[pl.BlockSpec((tm, 768), row),
                  pl.BlockSpec((1, MLA_Q_RANK), fixed),
                  pl.BlockSpec((1, MLA_KV_RANK), fixed),
                  pl.BlockSpec((MLA_Q_RANK, 2 * hw), fixed),
                  pl.BlockSpec((MLA_KV_RANK, hw), fixed),
                  pl.BlockSpec((GROUP_WIDTH, MLA_KV_RANK), fixed),
                  pl.BlockSpec((tm, LANES), table),
                  pl.BlockSpec((tm, LANES), table)],
        out_specs=[pl.BlockSpec((tm, hw), row),
                   pl.BlockSpec((tm, hw), row),
                   pl.BlockSpec((None, per_tile, MLA_HEADS * MLA_VROWS, ATT_TK),
                                lambda i: (i // per_seq, i % per_seq, 0, 0))],
        out_shape=[jax.ShapeDtypeStruct((t, hw), BF16),
                   jax.ShapeDtypeStruct((t, hw), BF16),
                   jax.ShapeDtypeStruct((bsz, seq // ATT_TK, MLA_HEADS * MLA_VROWS, ATT_TK), BF16)],
        compiler_params=_cparams(("parallel",)),
        name="mla_prep",
    )(x, q_g.reshape(1, -1).astype(F32), kv_g.reshape(1, -1).astype(F32), wq_packed, wk_packed, wv_t, cos, sin)


def _attn_kernel(q_ref, k_ref, vt_ref, o_ref):
    n_steps = vt_ref.shape[0]
    tq = q_ref.shape[0]
    tk = ATT_TK
    heads = range(q_ref.shape[1] // LANES)
    unroll = min(n_steps, ATT_UNROLL)
    qs = [q_ref[:, h * LANES:(h + 1) * LANES] for h in heads]

    def scores(h, j):
        r0 = pl.multiple_of(j * tk, tk)
        return _dot_nt(k_ref[pl.ds(r0, tk), h * LANES:(h + 1) * LANES], qs[h])

    def update(h, j, s, m, acc):
        m_new = jnp.maximum(m, jnp.max(s, axis=0, keepdims=True))
        p = jnp.exp2(s - m_new).astype(BF16)
        pv = _dot(vt_ref[j, h * MLA_VROWS:(h + 1) * MLA_VROWS, :], p)
        return m_new, acc * jnp.exp2(m - m_new) + pv

    def group(g, carry):
        base = g * unroll
        state = [list(carry[2 * h:2 * h + 2]) for h in heads]
        pending = [[scores(h, base + u) for h in heads] for u in range(min(ATT_LOOKAHEAD, unroll))]
        for u in range(unroll):
            s_cur = pending.pop(0)
            if u + ATT_LOOKAHEAD < unroll:
                pending.append([scores(h, base + u + ATT_LOOKAHEAD) for h in heads])
            for h in heads:
                state[h] = list(update(h, base + u, s_cur[h], *state[h]))
        return tuple(v for st in state for v in st)

    one = (jnp.full((1, tq), -jnp.inf, F32), jnp.zeros((MLA_VROWS, tq), F32))
    final = lax.fori_loop(0, n_steps // unroll, group, one * len(heads))
    accs = [final[2 * h + 1] for h in heads]
    out_t = jnp.concatenate([acc[:MLA_V, :] / acc[MLA_V:MLA_V + 1, :] for acc in accs], axis=0)
    o_ref[...] = out_t.T.astype(BF16)


def _attention(q, k, vt, bsz, seq):
    hw = MLA_HEADS * LANES
    q3, k3 = (a.reshape(bsz, seq, hw) for a in (q, k))
    tq = ATT_TQ
    hps = MLA_HEADS if seq // ATT_TK <= ATT_ALL_HEADS_MAX_STEPS else 2
    out = pl.pallas_call(
        _attn_kernel,
        grid=(bsz, MLA_HEADS // hps, seq // tq),
        in_specs=[pl.BlockSpec((None, tq, hps * LANES), lambda b, p, i: (b, i, p)),
                  pl.BlockSpec((None, seq, hps * LANES), lambda b, p, i: (b, 0, p)),
                  pl.BlockSpec((None, seq // ATT_TK, hps * MLA_VROWS, ATT_TK), lambda b, p, i: (b, 0, p, 0))],
        out_specs=pl.BlockSpec((None, tq, hps * MLA_V), lambda b, p, i: (b, i, p)),
        out_shape=jax.ShapeDtypeStruct((bsz, seq, GROUP_WIDTH), BF16),
        compiler_params=_cparams(("parallel", "parallel", "arbitrary"), 48),
        name="mla_attention",
    )(q3, k3, vt)
    return out.reshape(bsz * seq, GROUP_WIDTH)


def _out_proj_kernel(pre_ln, x_ref, lng_ref, lnb_ref, of_ref, ob_ref, z_ref, ng_ref, headmean_ref,
                     yb_ref, yc_ref, yd_ref, w_ref, g_ref, b_ref, o_ref):
    x = x_ref[...]
    if pre_ln:
        x = _layer_norm(x, lng_ref[...], lnb_ref[...])
    o = of_ref[...].astype(F32) + ob_ref[...].astype(F32)
    z = z_ref[...].astype(F32)
    ya = o * lax.rsqrt(_dot(o * o, headmean_ref[...]) + RMS_EPS) * ng_ref[...] * _silu(z)
    gw = GROUP_WIDTH
    mix = (_dot(ya.astype(BF16), w_ref[0:gw, :]) + _dot(yb_ref[...], w_ref[gw:2 * gw, :])
           + _dot(yc_ref[...], w_ref[2 * gw:3 * gw, :]) + _dot(yd_ref[...], w_ref[3 * gw:, :]))
    o_ref[...] = _layer_norm(DEEPNORM_ALPHA * x + mix, g_ref[...], b_ref[...])


def _out_proj(x, ln_in_g, ln_in_b, o_f, o_b, z, norm_g, y_b, y_c, y_d, w_out, ln_g, ln_b, pre_ln):
    t = x.shape[0]
    tm = ROW_TILE
    head = np.arange(GROUP_WIDTH) // DN_DK
    headmean = jnp.asarray((head[:, None] == head[None, :]).astype(np.float32) / DN_DK)
    ng = jnp.tile(norm_g.astype(F32), DN_HEADS).reshape(1, GROUP_WIDTH)
    row = lambda i: (i, 0)
    fixed = lambda i: (0, 0)
    quarter = pl.BlockSpec((tm, GROUP_WIDTH), row)
    vec = pl.BlockSpec((1, D_MODEL), fixed)
    return pl.pallas_call(
        functools.partial(_out_proj_kernel, pre_ln),
        grid=(t // tm,),
        in_specs=[pl.BlockSpec((tm, D_MODEL), row), vec, vec,
                  quarter, quarter, quarter,
                  pl.BlockSpec((1, GROUP_WIDTH), fixed),
                  pl.BlockSpec((GROUP_WIDTH, GROUP_WIDTH), fixed),
                  quarter, quarter, quarter,
                  pl.BlockSpec((D_MODEL, D_MODEL), fixed), vec, vec],
        out_specs=pl.BlockSpec((tm, D_MODEL), row),
        out_shape=jax.ShapeDtypeStruct((t, D_MODEL), F32),
        compiler_params=_cparams(("parallel",)),
        name="out_proj",
    )(x, ln_in_g, ln_in_b, o_f, o_b, z, ng, headmean, y_b, y_c, y_d, w_out.astype(BF16),
      ln_g.reshape(1, -1).astype(F32), ln_b.reshape(1, -1).astype(F32))


def _router_kernel(x_ref, whi_ref, wlo_ref, bias_ref, cls_ref, rank_ref, cnt_ref, carry_ref):
    tm = x_ref.shape[0]

    @pl.when(pl.program_id(0) == 0)
    def _():
        carry_ref[...] = jnp.zeros_like(carry_ref)

    x = x_ref[...]
    x_hi = x.astype(BF16)
    x_lo = (x - x_hi.astype(F32)).astype(BF16)
    logits = _dot_nt(whi_ref[...], x_hi) + (_dot_nt(whi_ref[...], x_lo) + _dot_nt(wlo_ref[...], x_hi))
    biased = _sigmoid(logits) + bias_ref[...]
    v = [biased[e:e + 1, :] for e in range(N_EXPERTS)]
    best = chosen = pair = None
    for g in range(N_EXPERT_GROUPS):
        m = v[g * EXPERTS_PER_GROUP:(g + 1) * EXPERTS_PER_GROUP]
        score = functools.reduce(jnp.maximum, [m[i] + m[j] for i, j in PAIRS])
        picked = []
        for e in range(EXPERTS_PER_GROUP):
            outranked = [(m[o] >= m[e]) if o < e else (m[o] > m[e]) for o in range(EXPERTS_PER_GROUP) if o != e]
            picked.append(functools.reduce(jnp.add, [c.astype(jnp.int32) for c in outranked]) < 2)
        first = jnp.where(picked[0], 0, jnp.where(picked[1], 1, 2))
        last = jnp.where(picked[3], 3, jnp.where(picked[2], 2, 1))
        pair_g = (first * (7 - first)) // 2 + (last - first - 1)
        if g == 0:
            best, chosen, pair = score, jnp.zeros_like(pair_g), pair_g
        else:
            better = score > best
            best = jnp.where(better, score, best)
            chosen = jnp.where(better, g, chosen)
            pair = jnp.where(better, pair_g, pair)
    cls_row = (chosen * len(PAIRS) + pair).astype(F32)
    onehot = (lax.broadcasted_iota(jnp.int32, (CLASS_ROWS, tm), 0).astype(F32) == cls_row)
    upto = (lax.broadcasted_iota(jnp.int32, (tm, tm), 0) <= lax.broadcasted_iota(jnp.int32, (tm, tm), 1))
    prefix = _dot(onehot.astype(BF16), upto.astype(BF16))
    carry = carry_ref[:, 0:1]
    rank = jnp.sum(jnp.where(onehot, prefix + carry, 0.0), axis=0, keepdims=True) - 1.0
    cls_ref[...] = jnp.broadcast_to(cls_row, (HALO, tm))
    rank_ref[...] = jnp.broadcast_to(rank, (HALO, tm))
    total = carry + prefix[:, tm - 1:tm]
    carry_ref[...] = jnp.broadcast_to(total, carry_ref.shape)
    cnt_ref[...] = jnp.broadcast_to(total, cnt_ref.shape)


def _router(x, router_w, router_bias):
    t = x.shape[0]
    tm = ROW_TILE
    w = router_w.astype(F32).T
    w_hi = w.astype(BF16)
    w_lo = (w - w_hi.astype(F32)).astype(BF16)
    bias = router_bias.astype(F32).reshape(N_EXPERTS, 1)
    fixed = lambda i: (0, 0)
    cls, rank, cnt = pl.pallas_call(
        _router_kernel,
        grid=(t // tm,),
        in_specs=[pl.BlockSpec((tm, D_MODEL), lambda i: (i, 0)),
                  pl.BlockSpec((N_EXPERTS, D_MODEL), fixed),
                  pl.BlockSpec((N_EXPERTS, D_MODEL), fixed),
                  pl.BlockSpec((N_EXPERTS, 1), fixed)],
        out_specs=[pl.BlockSpec((HALO, tm), lambda i: (0, i)),
                   pl.BlockSpec((HALO, tm), lambda i: (0, i)),
                   pl.BlockSpec((CLASS_ROWS, LANES), fixed)],
        out_shape=[jax.ShapeDtypeStruct((HALO, t), F32),
                   jax.ShapeDtypeStruct((HALO, t), F32),
                   jax.ShapeDtypeStruct((CLASS_ROWS, LANES), F32)],
        scratch_shapes=[pltpu.VMEM((CLASS_ROWS, LANES), F32)],
        compiler_params=_cparams(("arbitrary",)),
        name="router",
    )(x, w_hi, w_lo, bias)
    return cls[0].astype(jnp.int32), rank[0].astype(jnp.int32), cnt[:N_CLASSES, 0].astype(jnp.int32)


def _routing_tables(cls, rank, counts, n_tiles):
    tm = MOE_TILE
    tiles = (counts + tm - 1) // tm
    first_tile = jnp.cumsum(tiles) - tiles
    dest = first_tile[cls] * tm + rank
    tile_ids = jnp.arange(n_tiles, dtype=jnp.int32)
    tile_cls = jnp.sum((jnp.cumsum(tiles)[None, :] <= tile_ids[:, None]).astype(jnp.int32), axis=1)
    tile_cls = jnp.minimum(tile_cls, N_CLASSES - 1)
    used = (tile_ids < jnp.sum(tiles)).astype(jnp.int32)
    pair = np.asarray(PAIRS, np.int32)
    base = (tile_cls // len(PAIRS)) * EXPERTS_PER_GROUP
    e1 = base + jnp.asarray(pair[:, 0])[tile_cls % len(PAIRS)]
    e2 = base + jnp.asarray(pair[:, 1])[tile_cls % len(PAIRS)]
    return dest.astype(jnp.int32), e1.astype(jnp.int32), e2.astype(jnp.int32), used


def _dispatch_kernel(dest_ref, x_ref, zeros_ref, xs_ref, sem):
    del zeros_ref
    tr = x_ref.shape[0]

    def row_copy(r):
        return pltpu.make_async_copy(x_ref.at[pl.ds(r, 1), :], xs_ref.at[pl.ds(dest_ref[0, 0, r], 1), :], sem)

    for r in range(tr):
        row_copy(r).start(priority=r % 2)
    for r in range(tr):
        row_copy(r).wait()


def _dispatch(x, dest, slots):
    t = x.shape[0]
    tr = DMA_TILE
    dest3 = dest.reshape(t // tr, 1, tr)
    return pl.pallas_call(
        _dispatch_kernel,
        grid=(t // tr,),
        in_specs=[pl.BlockSpec((1, 1, tr), lambda i: (i, 0, 0), memory_space=pltpu.SMEM),
                  pl.BlockSpec((tr, D_MODEL), lambda i: (i, 0)),
                  pl.BlockSpec(memory_space=pl.ANY)],
        out_specs=pl.BlockSpec(memory_space=pl.ANY),
        out_shape=jax.ShapeDtypeStruct(slots.shape, F32),
        scratch_shapes=[pltpu.SemaphoreType.DMA],
        input_output_aliases={2: 0},
        compiler_params=_cparams(("arbitrary",)),
        name="moe_dispatch",
    )(dest3, x, slots)


def _moe_kernel(e1_ref, e2_ref, used_ref, x_ref, rw_ref, wg1_ref, wu1_ref, wd1_ref, wg2_ref, wu2_ref, wd2_ref, y_ref):
    i = pl.program_id(0)

    @pl.when(used_ref[i] == 1)
    def _():
        xb = x_ref[...].astype(BF16)
        scores = _sigmoid(_dot(xb, rw_ref[...]))
        lane = lax.broadcasted_iota(jnp.int32, scores.shape, 1)
        s1 = jnp.sum(jnp.where(lane == e1_ref[i], scores, 0.0), axis=-1, keepdims=True)
        s2 = jnp.sum(jnp.where(lane == e2_ref[i], scores, 0.0), axis=-1, keepdims=True)
        inv = 1.0 / (s1 + s2)
        h1 = _silu(_dot(xb, wg1_ref[...])) * _dot(xb, wu1_ref[...]) * (s1 * inv)
        h2 = _silu(_dot(xb, wg2_ref[...])) * _dot(xb, wu2_ref[...]) * (s2 * inv)
        y_ref[...] = _dot(h1.astype(BF16), wd1_ref[...]) + _dot(h2.astype(BF16), wd2_ref[...])

    @pl.when(used_ref[i] == 0)
    def _():
        y_ref[...] = jnp.zeros_like(y_ref)


def _moe(xs, e1, e2, used, router_w, w_gate, w_up, w_down):
    n_slots = xs.shape[0]
    tm = MOE_TILE
    rw = jnp.zeros((D_MODEL, LANES), BF16).at[:, :N_EXPERTS].set(router_w.astype(BF16))
    up = lambda sel: pl.BlockSpec((None, D_MODEL, EXPERT_FF), lambda i, e1, e2, u: ((e1, e2)[sel][i], 0, 0))
    down = lambda sel: pl.BlockSpec((None, EXPERT_FF, D_MODEL), lambda i, e1, e2, u: ((e1, e2)[sel][i], 0, 0))
    grid_spec = pltpu.PrefetchScalarGridSpec(
        num_scalar_prefetch=3,
        grid=(n_slots // tm,),
        in_specs=[pl.BlockSpec((tm, D_MODEL), lambda i, e1, e2, u: (i, 0)),
                  pl.BlockSpec((D_MODEL, LANES), lambda i, e1, e2, u: (0, 0)),
                  up(0), up(0), down(0), up(1), up(1), down(1)],
        out_specs=pl.BlockSpec((tm, D_MODEL), lambda i, e1, e2, u: (i, 0)),
    )
    return pl.pallas_call(
        _moe_kernel,
        grid_spec=grid_spec,
        out_shape=jax.ShapeDtypeStruct((n_slots, D_MODEL), F32),
        compiler_params=_cparams(("arbitrary",), 48),
        name="moe_experts",
    )(e1, e2, used, xs, rw, w_gate, w_up, w_down, w_gate, w_up, w_down)


def _combine_kernel(dest_ref, dest_next_ref, x_ref, ys_ref, g_ref, b_ref, o_ref, buf, sem):
    tr = x_ref.shape[0]
    i = pl.program_id(0)
    slot = i % 2

    def row_copy(idx_ref, r, s):
        return pltpu.make_async_copy(ys_ref.at[pl.ds(idx_ref[0, 0, r], 1), :], buf.at[s, pl.ds(r, 1), :], sem.at[s])

    @pl.when(i == 0)
    def _():
        for r in range(tr):
            row_copy(dest_ref, r, slot).start(priority=r % 2)

    @pl.when(i + 1 < pl.num_programs(0))
    def _():
        for r in range(tr):
            row_copy(dest_next_ref, r, 1 - slot).start(priority=r % 2)

    for r in range(tr):
        row_copy(dest_ref, r, slot).wait()
    o_ref[...] = _layer_norm(DEEPNORM_ALPHA * x_ref[...] + buf[slot], g_ref[...], b_ref[...])


def _combine(x, ys, dest, ln_g, ln_b):
    t = x.shape[0]
    tr = DMA_TILE
    n = t // tr
    dest3 = dest.reshape(n, 1, tr)
    fixed = lambda i: (0, 0)
    return pl.pallas_call(
        _combine_kernel,
        grid=(n,),
        in_specs=[pl.BlockSpec((1, 1, tr), lambda i: (i, 0, 0), memory_space=pltpu.SMEM),
                  pl.BlockSpec((1, 1, tr), lambda i: (jnp.minimum(i + 1, n - 1), 0, 0), memory_space=pltpu.SMEM),
                  pl.BlockSpec((tr, D_MODEL), lambda i: (i, 0)),
                  pl.BlockSpec(memory_space=pl.ANY),
                  pl.BlockSpec((1, D_MODEL), fixed),
                  pl.BlockSpec((1, D_MODEL), fixed)],
        out_specs=pl.BlockSpec((tr, D_MODEL), lambda i: (i, 0)),
        out_shape=jax.ShapeDtypeStruct((t, D_MODEL), F32),
        scratch_shapes=[pltpu.VMEM((2, tr, D_MODEL), F32), pltpu.SemaphoreType.DMA((2,))],
        compiler_params=_cparams(("arbitrary",)),
        name="moe_combine",
    )(dest3, dest3, x, ys, ln_g.reshape(1, -1).astype(F32), ln_b.reshape(1, -1).astype(F32))


def _trunk(x, p, slots):
    bsz, seq, d = x.shape
    t = bsz * seq
    x = x.reshape(t, d)
    cos, sin = _rope_tables(seq)
    n_tiles = t // MOE_TILE + N_CLASSES
    for l in range(DEPTH):
        first = l == 0
        qkv, z, sgu_in, pool_in, mla_in, misc = _in_proj(x, p["ln_in_g"], p["ln_in_b"], p["w_in"][l], first)
        qkvn, gb = _dn_prep(qkv, misc, p["dn_conv_w"][l], p["dn_a_log"][l], p["dn_dt_bias"][l], bsz, seq)
        o_f, o_b = _dn_main(qkvn, gb, bsz, seq)
        y_b = _sgu(sgu_in, p["sgu_ln_g"][l], p["sgu_ln_b"][l], p["sgu_w"][l], p["sgu_b"][l], bsz, seq)
        y_c = _pool(pool_in, p["pool_w"][l], p["pool_scale"][l], bsz, seq)
        q, k, vt = _mla_prep(mla_in, p["mla_q_norm_g"][l], p["mla_kv_norm_g"][l], p["wq"][l], p["wk"][l],
                             p["wvt"][l], cos, sin, bsz, seq)
        y_d = _attention(q, k, vt, bsz, seq)
        x = _out_proj(x, p["ln_in_g"], p["ln_in_b"], o_f.reshape(t, -1), o_b.reshape(t, -1), z,
                      p["dn_norm_g"][l], y_b, y_c, y_d, p["w_out"][l], p["ln1_g"][l], p["ln1_b"][l], first)
        cls, rank, counts = _router(x, p["router_w"], p["router_bias"])
        dest, e1, e2, used = _routing_tables(cls, rank, counts, n_tiles)
        if slots is None or slots.shape[0] != n_tiles * MOE_TILE:
            slots = jnp.zeros((n_tiles * MOE_TILE, D_MODEL), F32)
        slots = _dispatch(x, dest, slots)
        ys = _moe(slots, e1, e2, used, p["router_w"], p["moe_w_gate"][l], p["moe_w_up"][l], p["moe_w_down"][l])
        x = _combine(x, ys, dest, p["ln2_g"][l], p["ln2_b"][l])
    return x.reshape(bsz, seq, d), slots


def kernel(x_prompt, x_sample, ln_in_g, ln_in_b, w_in, dn_conv_w, dn_a_log, dn_dt_bias, dn_norm_g, sgu_ln_g, sgu_ln_b, sgu_w, sgu_b, pool_w, pool_scale, mla_q_norm_g, mla_kv_norm_g, mla_w_uq, mla_w_uk, mla_w_uv, w_out, ln1_g, ln1_b, router_w, router_bias, moe_w_gate, moe_w_up, moe_w_down, ln2_g, ln2_b):
    packed_mla = [_pack_mla_weights(mla_w_uq[l], mla_w_uk[l], mla_w_uv[l]) for l in range(DEPTH)]
    p = dict(
        ln_in_g=ln_in_g.reshape(1, -1).astype(F32), ln_in_b=ln_in_b.reshape(1, -1).astype(F32),
        w_in=[_pack_in_weights(w_in[l]) for l in range(DEPTH)],
        dn_conv_w=dn_conv_w, dn_a_log=dn_a_log, dn_dt_bias=dn_dt_bias, dn_norm_g=dn_norm_g,
        sgu_ln_g=sgu_ln_g, sgu_ln_b=sgu_ln_b, sgu_w=sgu_w, sgu_b=sgu_b,
        pool_w=pool_w, pool_scale=pool_scale,
        mla_q_norm_g=mla_q_norm_g, mla_kv_norm_g=mla_kv_norm_g,
        wq=[w[0] for w in packed_mla], wk=[w[1] for w in packed_mla], wvt=[w[2] for w in packed_mla],
        w_out=w_out, ln1_g=ln1_g, ln1_b=ln1_b, router_w=router_w, router_bias=router_bias,
        moe_w_gate=moe_w_gate.astype(BF16), moe_w_up=moe_w_up.astype(BF16), moe_w_down=moe_w_down.astype(BF16),
        ln2_g=ln2_g, ln2_b=ln2_b,
    )
    y_prompt, slots = _trunk(x_prompt, p, None)
    y_sample, _ = _trunk(x_sample, p, slots)
    return y_prompt, y_sample
```
